```python
import jax, jax.numpy as jnp
from jax import lax
import numpy as np

D_MODEL = 1024
BATCH = 1
SEQ = 16384
DEPTH = 4
DEC_BATCH = 4
DEC_SEQ = 8192
PAST_LEN = 128

N_MIXERS = 2
N_CONV_LAYERS = (DEPTH + 1) // 2
N_ATTN_LAYERS = DEPTH // 2
HEAD_DIM = 128
N_HEADS = D_MODEL // HEAD_DIM
N_KV_HEADS = 2
GROUP = N_HEADS // N_KV_HEADS
ROPE_AXIS_DIM = HEAD_DIM // 2
ROPE_THETA = 10000.0
GRID_W = 64
Q_BLOCK = 128
CONV_WIDTH = 3
D_FF = 2816
LN_EPS = 1e-5
QK_EPS = 1e-6
DEEPNORM_ALPHA = (2.0 * DEPTH) ** 0.25
DEEPNORM_BETA = (8.0 * DEPTH) ** -0.25

kernel_name = "hybrid_conv_axialrope_gqa_macaron_deepnorm_encoder"


def _layer_norm(x, g, b):
    xf = x.astype(jnp.float32)
    mu = jnp.mean(xf, axis=-1, keepdims=True)
    xc = xf - mu
    var = jnp.mean(xc * xc, axis=-1, keepdims=True)
    y = xc * lax.rsqrt(var + LN_EPS) * g.astype(jnp.float32) + b.astype(jnp.float32)
    return y.astype(x.dtype)


def _swiglu(x, w_in, w_out):
    gu = x @ w_in
    g, u = jnp.split(gu, 2, axis=-1)
    return (jax.nn.silu(g) * u) @ w_out


def _short_conv_mixer(x, w_in, k, w_out):
    s = x.shape[1]
    bgate, cgate, h = jnp.split(x @ w_in, 3, axis=-1)
    v = cgate * h
    pad = CONV_WIDTH // 2
    vp = jnp.pad(v, ((0, 0), (pad, pad), (0, 0)))
    conv = k[0] * vp[:, 0:s]
    for j in range(1, CONV_WIDTH):
        conv = conv + k[j] * vp[:, j:j + s]
    return (bgate * conv) @ w_out


def _rms_head(x, g):
    xf = x.astype(jnp.float32)
    return xf * lax.rsqrt(jnp.mean(xf * xf, axis=-1, keepdims=True) + QK_EPS) * g.astype(jnp.float32)


def _rope_axis(x, pos):
    d = x.shape[-1]
    inv_freq = ROPE_THETA ** (-jnp.arange(0, d, 2, dtype=jnp.float32) / d)
    ang = pos.astype(jnp.float32)[:, None] * inv_freq[None, :]
    cos = jnp.cos(ang)[None, :, None, :]
    sin = jnp.sin(ang)[None, :, None, :]
    x1, x2 = jnp.split(x, 2, axis=-1)
    return jnp.concatenate([x1 * cos - x2 * sin, x2 * cos + x1 * sin], axis=-1)


def _axial_rope(x):
    s = x.shape[1]
    t = jnp.arange(s, dtype=jnp.int32)
    rows = t // GRID_W
    cols = t % GRID_W
    return jnp.concatenate([_rope_axis(x[..., :ROPE_AXIS_DIM], rows),
                            _rope_axis(x[..., ROPE_AXIS_DIM:], cols)], axis=-1)


def _attention_mixer(x, w_qkv, q_norm, k_norm, w_out):
    bsz, s, _ = x.shape
    qkv = x @ w_qkv
    q, k, v = jnp.split(qkv, [N_HEADS * HEAD_DIM, (N_HEADS + N_KV_HEADS) * HEAD_DIM], axis=-1)
    q = q.reshape(bsz, s, N_HEADS, HEAD_DIM)
    k = k.reshape(bsz, s, N_KV_HEADS, HEAD_DIM)
    v = v.reshape(bsz, s, N_KV_HEADS, HEAD_DIM).astype(jnp.float32)
    q = _axial_rope(_rms_head(q, q_norm)) * (HEAD_DIM ** -0.5)
    k = _axial_rope(_rms_head(k, k_norm))
    n_blk = s // Q_BLOCK
    qb = q.reshape(bsz, n_blk, Q_BLOCK, N_KV_HEADS, GROUP, HEAD_DIM).transpose(1, 0, 2, 3, 4, 5)

    def block(q_blk):
        scores = jnp.einsum('bqkgd,bskd->bkgqs', q_blk, k)
        p = jax.nn.softmax(scores, axis=-1)
        return jnp.einsum('bkgqs,bskd->bqkgd', p, v)

    o = lax.map(block, qb)
    o = o.transpose(1, 0, 2, 3, 4, 5).reshape(bsz, s, N_HEADS * HEAD_DIM).astype(x.dtype)
    return o @ w_out


def _trunk(x, ffn1_w_in, ffn1_w_out, ffn2_w_in, ffn2_w_out, ln_g, ln_b,
           conv_w_in, conv_k, conv_w_out, attn_w_qkv, attn_q_norm, attn_k_norm, attn_w_out):
    for i in range(DEPTH):
        x = _layer_norm(DEEPNORM_ALPHA * x + 0.5 * _swiglu(x, ffn1_w_in[i], ffn1_w_out[i]), ln_g[i, 0], ln_b[i, 0])
        j = i // N_MIXERS
        if i % N_MIXERS == 0:
            mix = _short_conv_mixer(x, conv_w_in[j], conv_k[j], conv_w_out[j])
        else:
            mix = _attention_mixer(x, attn_w_qkv[j], attn_q_norm[j], attn_k_norm[j], attn_w_out[j])
        x = _layer_norm(DEEPNORM_ALPHA * x + mix, ln_g[i, 1], ln_b[i, 1])
        x = _layer_norm(DEEPNORM_ALPHA * x + 0.5 * _swiglu(x, ffn2_w_in[i], ffn2_w_out[i]), ln_g[i, 2], ln_b[i, 2])
    return x


def setup_inputs(seed: int = 0) -> dict:
    key = jax.random.key(seed)
    ks = jax.random.split(key, 16)
    f32 = jnp.float32
    qkv_out = (N_HEADS + 2 * N_KV_HEADS) * HEAD_DIM

    def nrm(k, shape, scale):
        return jax.random.normal(k, shape, dtype=f32) * scale

    return {
        "x_prompt": nrm(ks[0], (BATCH, SEQ, D_MODEL), 1.0),
        "x_sample": nrm(ks[1], (DEC_BATCH, DEC_SEQ, D_MODEL), 1.0),
        "ffn1_w_in": nrm(ks[2], (DEPTH, D_MODEL, 2 * D_FF), D_MODEL ** -0.5),
        "ffn1_w_out": nrm(ks[3], (DEPTH, D_FF, D_MODEL), DEEPNORM_BETA * D_FF ** -0.5),
        "ffn2_w_in": nrm(ks[4], (DEPTH, D_MODEL, 2 * D_FF), D_MODEL ** -0.5),
        "ffn2_w_out": nrm(ks[5], (DEPTH, D_FF, D_MODEL), DEEPNORM_BETA * D_FF ** -0.5),
        "ln_g": 1.0 + nrm(ks[6], (DEPTH, 3, D_MODEL), 0.02),
        "ln_b": nrm(ks[7], (DEPTH, 3, D_MODEL), 0.02),
        "conv_w_in": nrm(ks[8], (N_CONV_LAYERS, D_MODEL, 3 * D_MODEL), D_MODEL ** -0.5),
        "conv_k": nrm(ks[9], (N_CONV_LAYERS, CONV_WIDTH, D_MODEL), CONV_WIDTH ** -0.5),
        "conv_w_out": nrm(ks[10], (N_CONV_LAYERS, D_MODEL, D_MODEL), DEEPNORM_BETA * D_MODEL ** -0.5),
        "attn_w_qkv": nrm(ks[11], (N_ATTN_LAYERS, D_MODEL, qkv_out), D_MODEL ** -0.5),
        "attn_q_norm": 1.0 + nrm(ks[12], (N_ATTN_LAYERS, HEAD_DIM), 0.02),
        "attn_k_norm": 1.0 + nrm(ks[13], (N_ATTN_LAYERS, HEAD_DIM), 0.02),
        "attn_w_out": nrm(ks[14], (N_ATTN_LAYERS, N_HEADS * HEAD_DIM, D_MODEL), DEEPNORM_BETA * (N_HEADS * HEAD_DIM) ** -0.5),
    }


def reference(x_prompt, x_sample, ffn1_w_in, ffn1_w_out, ffn2_w_in, ffn2_w_out, ln_g, ln_b,
              conv_w_in, conv_k, conv_w_out, attn_w_qkv, attn_q_norm, attn_k_norm, attn_w_out):
    y_prompt = _trunk(x_prompt, ffn1_w_in, ffn1_w_out, ffn2_w_in, ffn2_w_out, ln_g, ln_b,
                      conv_w_in, conv_k, conv_w_out, attn_w_qkv, attn_q_norm, attn_k_norm, attn_w_out)
    y_sample = _trunk(x_sample, ffn1_w_in, ffn1_w_out, ffn2_w_in, ffn2_w_out, ln_g, ln_b,
                      conv_w_in, conv_k, conv_w_out, attn_w_qkv, attn_q_norm, attn_k_norm, attn_w_out)
    return (y_prompt, y_sample)
```

```python
import functools

import jax
import jax.numpy as jnp
import numpy as np
from jax import lax
from jax.experimental import pallas as pl
from jax.experimental.pallas import tpu as pltpu

DEPTH = 4
HEAD_DIM = 128
N_KV_HEADS = 2
GROUP = 4
N_HEADS = N_KV_HEADS * GROUP
ROPE_AXIS_DIM = HEAD_DIM // 2
ROPE_THETA = 10000.0
GRID_W = 64
CONV_WIDTH = 3
LN_EPS = 1e-5
QK_EPS = 1e-6
DEEPNORM_ALPHA = (2.0 * DEPTH) ** 0.25

LANES = 128
SUBLANES = 8
MXU_DIM = 256
VMEM_LIMIT_BYTES = 56 * 1024 * 1024

TOKEN_TILE = 512
FFN_CHUNK = MXU_DIM
Q_TILE = 256
KV_TILE = 512

F32 = jnp.float32
BF16 = jnp.bfloat16


def _compiler_params(n_grid):
    return pltpu.CompilerParams(dimension_semantics=("arbitrary",) * n_grid,
                                vmem_limit_bytes=VMEM_LIMIT_BYTES)


def _layer_norm(y, g, b):
    mu = jnp.mean(y, axis=-1, keepdims=True)
    yc = y - mu
    var = jnp.mean(yc * yc, axis=-1, keepdims=True)
    return yc * lax.rsqrt(var + LN_EPS) * g + b


def _resident(shape):
    return pl.BlockSpec(shape, lambda *_: (0,) * len(shape))


def _ffn_kernel(x_ref, win_ref, wout_ref, g_ref, b_ref, o_ref, *, n_chunks, chunk):
    x = x_ref[...]
    xb = x.astype(BF16)
    acc = jnp.zeros(x.shape, F32)
    for c in range(n_chunks):
        gu = jnp.dot(xb, win_ref[:, c * 2 * chunk:(c + 1) * 2 * chunk],
                     preferred_element_type=F32)
        gate = gu[:, :chunk]
        up = gu[:, chunk:]
        h = gate / (1.0 + jnp.exp(-gate)) * up
        acc = acc + jnp.dot(h.astype(BF16), wout_ref[c * chunk:(c + 1) * chunk, :],
                            preferred_element_type=F32)
    y = DEEPNORM_ALPHA * x + 0.5 * acc
    o_ref[...] = _layer_norm(y, g_ref[...], b_ref[...])


def _ffn(x, win, wout, g, b, *, tm):
    n, d = x.shape
    dff = wout.shape[0]
    chunk = FFN_CHUNK
    kern = functools.partial(_ffn_kernel, n_chunks=dff // chunk, chunk=chunk)
    return pl.pallas_call(
        kern,
        grid=(n // tm,),
        in_specs=[pl.BlockSpec((tm, d), lambda i: (i, 0)),
                  _resident(win.shape), _resident(wout.shape),
                  _resident(g.shape), _resident(b.shape)],
        out_specs=pl.BlockSpec((tm, d), lambda i: (i, 0)),
        out_shape=jax.ShapeDtypeStruct((n, d), F32),
        compiler_params=_compiler_params(1),
        name="ffn",
    )(x, win, wout, g, b)


def _conv_kernel(x_ref, xprev_ref, xnext_ref, win_ref, ck_ref, wout_ref, g_ref, b_ref,
                 o_ref, v_ref, *, tiles_per_seq):
    i = pl.program_id(0)
    tm, d = x_ref.shape
    t = i % tiles_per_seq
    x = x_ref[...]
    xp = jnp.where(t == 0, 0.0, xprev_ref[...])
    xn = jnp.where(t == tiles_per_seq - 1, 0.0, xnext_ref[...])
    xcat = jnp.concatenate([xp, x, xn], axis=0).astype(BF16)
    ch = jnp.dot(xcat, win_ref[:, d:], preferred_element_type=F32)
    v_ref[...] = ch[:, :d] * ch[:, d:]
    bgate = jnp.dot(x.astype(BF16), win_ref[:, :d], preferred_element_type=F32)
    ck = ck_ref[...]
    conv = (ck[0:1] * v_ref[SUBLANES - 1:SUBLANES - 1 + tm, :]
            + ck[1:2] * v_ref[SUBLANES:SUBLANES + tm, :]
            + ck[2:3] * v_ref[SUBLANES + 1:SUBLANES + 1 + tm, :])
    mix = jnp.dot((bgate * conv).astype(BF16), wout_ref[...], preferred_element_type=F32)
    o_ref[...] = _layer_norm(DEEPNORM_ALPHA * x + mix, g_ref[...], b_ref[...])


def _conv_mixer(x, win, ck, wout, g, b, *, seq, tm):
    n, d = x.shape
    rb = tm // SUBLANES
    last_rb = n // SUBLANES - 1
    kern = functools.partial(_conv_kernel, tiles_per_seq=seq // tm)
    return pl.pallas_call(
        kern,
        grid=(n // tm,),
        in_specs=[pl.BlockSpec((tm, d), lambda i: (i, 0)),
                  pl.BlockSpec((SUBLANES, d), lambda i: (jnp.maximum(i * rb - 1, 0), 0)),
                  pl.BlockSpec((SUBLANES, d), lambda i: (jnp.minimum((i + 1) * rb, last_rb), 0)),
                  _resident(win.shape), _resident(ck.shape), _resident(wout.shape),
                  _resident(g.shape), _resident(b.shape)],
        out_specs=pl.BlockSpec((tm, d), lambda i: (i, 0)),
        out_shape=jax.ShapeDtypeStruct((n, d), F32),
        scratch_shapes=[pltpu.VMEM((tm + 2 * SUBLANES, d), F32)],
        compiler_params=_compiler_params(1),
        name="conv_mixer",
    )(x, x, x, win, ck, wout, g, b)


def _rope_tables(seq):
    t = jnp.arange(seq, dtype=jnp.int32)
    half = ROPE_AXIS_DIM // 2
    inv_freq = ROPE_THETA ** (-jnp.arange(0, ROPE_AXIS_DIM, 2, dtype=F32) / ROPE_AXIS_DIM)
    cos_parts, sin_parts = [], []
    for pos in (t // GRID_W, t % GRID_W):
        ang = pos.astype(F32)[:, None] * inv_freq[None, :]
        assert ang.shape == (seq, half)
        cos_parts += [jnp.cos(ang), jnp.cos(ang)]
        sin_parts += [-jnp.sin(ang), jnp.sin(ang)]
    return jnp.concatenate(cos_parts, axis=-1), jnp.concatenate(sin_parts, axis=-1)


def _qkv_kernel(x_ref, w_ref, qg_ref, kg_ref, cos_ref, sin_ref, q_ref, k_ref, v_ref):
    qkv = jnp.dot(x_ref[...].astype(BF16), w_ref[...], preferred_element_type=F32)
    cos = cos_ref[...]
    sin = sin_ref[...]
    half = ROPE_AXIS_DIM // 2
    lane = lax.broadcasted_iota(jnp.int32, cos.shape, 1)
    first_half = (lane % ROPE_AXIS_DIM) < half

    def norm_rope(seg, gain):
        ms = jnp.mean(seg * seg, axis=-1, keepdims=True)
        y = seg * lax.rsqrt(ms + QK_EPS) * gain
        partner = jnp.where(first_half,
                            pltpu.roll(y, HEAD_DIM - half, axis=1),
                            pltpu.roll(y, half, axis=1))
        return y * cos + partner * sin

    qg = qg_ref[...]
    kg = kg_ref[...]
    for h in range(N_HEADS):
        seg = qkv[:, h * HEAD_DIM:(h + 1) * HEAD_DIM]
        q_ref[:, h * HEAD_DIM:(h + 1) * HEAD_DIM] = (
            norm_rope(seg, qg) * (HEAD_DIM ** -0.5)).astype(BF16)
    for h in range(N_KV_HEADS):
        o = (N_HEADS + h) * HEAD_DIM
        k_ref[:, h * HEAD_DIM:(h + 1) * HEAD_DIM] = norm_rope(qkv[:, o:o + HEAD_DIM], kg).astype(BF16)
    v_ref[...] = qkv[:, (N_HEADS + N_KV_HEADS) * HEAD_DIM:].astype(BF16)


def _qkv(x, w, qg, kg, cos, sin, *, seq, tm):
    n, d = x.shape
    tiles_per_seq = seq // tm
    dq = N_HEADS * HEAD_DIM
    dkv = N_KV_HEADS * HEAD_DIM
    return pl.pallas_call(
        _qkv_kernel,
        grid=(n // tm,),
        in_specs=[pl.BlockSpec((tm, d), lambda i: (i, 0)),
                  _resident(w.shape), _resident(qg.shape), _resident(kg.shape),
                  pl.BlockSpec((tm, HEAD_DIM), lambda i: (i % tiles_per_seq, 0)),
                  pl.BlockSpec((tm, HEAD_DIM), lambda i: (i % tiles_per_seq, 0))],
        out_specs=[pl.BlockSpec((tm, dq), lambda i: (i, 0)),
                   pl.BlockSpec((tm, dkv), lambda i: (i, 0)),
                   pl.BlockSpec((tm, dkv), lambda i: (i, 0))],
        out_shape=[jax.ShapeDtypeStruct((n, dq), BF16),
                   jax.ShapeDtypeStruct((n, dkv), BF16),
                   jax.ShapeDtypeStruct((n, dkv), BF16)],
        compiler_params=_compiler_params(1),
        name="qkv_rope",
    )(x, w, qg, kg, cos, sin)


def _flash_kernel(q_ref, k_ref, v_ref, o_ref, m_ref, l_ref, acc_ref, *, tk):
    tq = q_ref.shape[0]
    seq = k_ref.shape[0]
    q = jnp.concatenate([q_ref[:, h * HEAD_DIM:(h + 1) * HEAD_DIM] for h in range(GROUP)], axis=0)
    m_ref[...] = jnp.full(m_ref.shape, -jnp.inf, F32)
    l_ref[...] = jnp.zeros(l_ref.shape, F32)
    acc_ref[...] = jnp.zeros(acc_ref.shape, F32)

    def step(j, carry):
        start = pl.multiple_of(j * tk, tk)
        kb = k_ref[pl.ds(start, tk), :]
        vb = v_ref[pl.ds(start, tk), :]
        s = lax.dot_general(q, kb, (((1,), (1,)), ((), ())), preferred_element_type=F32)
        blocks = [s[:, c * LANES:(c + 1) * LANES] for c in range(tk // LANES)]
        mx = blocks[0]
        for blk in blocks[1:]:
            mx = jnp.maximum(mx, blk)
        m_prev = m_ref[...]
        m_next = jnp.maximum(m_prev, jnp.max(mx, axis=1, keepdims=True))
        alpha = jnp.exp(m_prev - m_next)
        ps = [jnp.exp(blk - m_next) for blk in blocks]
        psum = ps[0]
        for pc in ps[1:]:
            psum = psum + pc
        p = jnp.concatenate(ps, axis=1).astype(BF16)
        l_ref[...] = alpha * l_ref[...] + psum
        acc_ref[...] = alpha * acc_ref[...] + jnp.dot(p, vb, preferred_element_type=F32)
        m_ref[...] = m_next
        return carry

    lax.fori_loop(0, seq // tk, step, 0)
    out = acc_ref[...] / jnp.sum(l_ref[...], axis=1, keepdims=True)
    for h in range(GROUP):
        o_ref[:, h * HEAD_DIM:(h + 1) * HEAD_DIM] = out[h * tq:(h + 1) * tq].astype(BF16)


def _flash(q, k, v, *, batch, seq, tq, tk):
    n = q.shape[0]
    gw = GROUP * HEAD_DIM
    q_tiles = seq // tq
    kern = functools.partial(_flash_kernel, tk=tk)
    return pl.pallas_call(
        kern,
        grid=(batch, N_KV_HEADS, q_tiles),
        in_specs=[pl.BlockSpec((tq, gw), lambda b, g, i: (b * q_tiles + i, g)),
                  pl.BlockSpec((seq, HEAD_DIM), lambda b, g, i: (b, g)),
                  pl.BlockSpec((seq, HEAD_DIM), lambda b, g, i: (b, g))],
        out_specs=pl.BlockSpec((tq, gw), lambda b, g, i: (b * q_tiles + i, g)),
        out_shape=jax.ShapeDtypeStruct((n, N_HEADS * HEAD_DIM), BF16),
        scratch_shapes=[pltpu.VMEM((GROUP * tq, LANES), F32),
                        pltpu.VMEM((GROUP * tq, LANES), F32),
                        pltpu.VMEM((GROUP * tq, HEAD_DIM), F32)],
        compiler_params=_compiler_params(3),
        name="flash_gqa",
    )(q, k, v)


def _proj_ln_kernel(a_ref, x_ref, w_ref, g_ref, b_ref, o_ref):
    mix = jnp.dot(a_ref[...], w_ref[...], preferred_element_type=F32)
    o_ref[...] = _layer_norm(DEEPNORM_ALPHA * x_ref[...] + mix, g_ref[...], b_ref[...])


def _proj_ln(a, x, w, g, b, *, tm):
    n, d = x.shape
    return pl.pallas_call(
        _proj_ln_kernel,
        grid=(n // tm,),
        in_specs=[pl.BlockSpec((tm, a.shape[1]), lambda i: (i, 0)),
                  pl.BlockSpec((tm, d), lambda i: (i, 0)),
                  _resident(w.shape), _resident(g.shape), _resident(b.shape)],
        out_specs=pl.BlockSpec((tm, d), lambda i: (i, 0)),
        out_shape=jax.ShapeDtypeStruct((n, d), F32),
        compiler_params=_compiler_params(1),
        name="attn_out_ln",
    )(a, x, w, g, b)


def _pack_ffn_in(w_in):
    nl, d, two_dff = w_in.shape
    dff = two_dff // 2
    w = w_in.reshape(nl, d, 2, dff // FFN_CHUNK, FFN_CHUNK)
    return w.transpose(0, 1, 3, 2, 4).reshape(nl, d, two_dff).astype(BF16)


def _trunk(x, p):
    batch, seq, d = x.shape
    tm = min(TOKEN_TILE, seq)
    tq = min(Q_TILE, seq)
    tk = min(KV_TILE, seq)
    assert seq % tm == 0 and seq % tq == 0 and seq % tk == 0 and seq % GRID_W == 0
    x = x.reshape(batch * seq, d)
    cos, sin = _rope_tables(seq)
    for i in range(DEPTH):
        g = p["ln_g"][i][:, None, :]
        b = p["ln_b"][i][:, None, :]
        x = _ffn(x, p["ffn1_w_in"][i], p["ffn1_w_out"][i], g[0], b[0], tm=tm)
        j = i // 2
        if i % 2 == 0:
            x = _conv_mixer(x, p["conv_w_in"][j], p["conv_k"][j], p["conv_w_out"][j],
                            g[1], b[1], seq=seq, tm=tm)
        else:
            q, k, v = _qkv(x, p["attn_w_qkv"][j], p["attn_q_norm"][j][None, :],
                           p["attn_k_norm"][j][None, :], cos, sin, seq=seq, tm=tm)
            o = _flash(q, k, v, batch=batch, seq=seq, tq=tq, tk=tk)
            x = _proj_ln(o, x, p["attn_w_out"][j], g[1], b[1], tm=tm)
        x = _ffn(x, p["ffn2_w_in"][i], p["ffn2_w_out"][i], g[2], b[2], tm=tm)
    return x.reshape(batch, seq, d)


def kernel(x_prompt, x_sample, ffn1_w_in, ffn1_w_out, ffn2_w_in, ffn2_w_out, ln_g, ln_b,
           conv_w_in, conv_k, conv_w_out, attn_w_qkv, attn_q_norm, attn_k_norm, attn_w_out):
    p = {
        "ffn1_w_in": _pack_ffn_in(ffn1_w_in), "ffn1_w_out": ffn1_w_out.astype(BF16),
        "ffn2_w_in": _pack_ffn_in(ffn2_w_in), "ffn2_w_out": ffn2_w_out.astype(BF16),
        "ln_g": ln_g, "ln_b": ln_b,
        "conv_w_in": conv_w_in.astype(BF16), "conv_k": conv_k, "conv_w_out": conv_w_out.astype(BF16),
        "attn_w_qkv": attn_w_qkv.astype(BF16), "attn_q_norm": attn_q_norm,
        "attn_k_norm": attn_k_norm, "attn_w_out": attn_w_out.astype(BF16),
    }
    return (_trunk(x_prompt, p), _trunk(x_sample, p))
```

```python
import functools

import jax
import jax.numpy as jnp
import numpy as np
from jax import lax
from jax.experimental import pallas as pl
from jax.experimental.pallas import tpu as pltpu

DEPTH = 4
HEAD_DIM = 128
N_KV_HEADS = 2
GROUP = 4
N_HEADS = N_KV_HEADS * GROUP
ROPE_AXIS_DIM = HEAD_DIM // 2
ROPE_THETA = 10000.0
GRID_W = 64
CONV_WIDTH = 3
LN_EPS = 1e-5
QK_EPS = 1e-6
DEEPNORM_ALPHA = (2.0 * DEPTH) ** 0.25
LOG2_E = 1.4426950408889634

LANES = 128
SUBLANES = 8
MXU_DIM = 256
VMEM_LIMIT_BYTES = 56 * 1024 * 1024

TOKEN_TILE = 512
FFN_CHUNK = MXU_DIM
Q_TILE = 256
KV_TILE = 512

F32 = jnp.float32
BF16 = jnp.bfloat16


def _compiler_params(n_grid):
    return pltpu.CompilerParams(dimension_semantics=("arbitrary",) * n_grid,
                                vmem_limit_bytes=VMEM_LIMIT_BYTES)


def _layer_norm(y, g, b):
    mu = jnp.mean(y, axis=-1, keepdims=True)
    yc = y - mu
    var = jnp.mean(yc * yc, axis=-1, keepdims=True)
    return yc * lax.rsqrt(var + LN_EPS) * g + b


def _resident(shape):
    return pl.BlockSpec(shape, lambda *_: (0,) * len(shape))


def _ffn_kernel(x_ref, win_ref, wout_ref, g_ref, b_ref, o_ref, *, n_chunks, chunk):
    x = x_ref[...]
    xb = x.astype(BF16)
    acc = jnp.zeros(x.shape, F32)
    for c in range(n_chunks):
        gu = jnp.dot(xb, win_ref[:, c * 2 * chunk:(c + 1) * 2 * chunk],
                     preferred_element_type=F32)
        gate = gu[:, :chunk]
        up = gu[:, chunk:]
        h = gate / (1.0 + jnp.exp(-gate)) * up
        acc = acc + jnp.dot(h.astype(BF16), wout_ref[c * chunk:(c + 1) * chunk, :],
                            preferred_element_type=F32)
    y = DEEPNORM_ALPHA * x + 0.5 * acc
    o_ref[...] = _layer_norm(y, g_ref[...], b_ref[...])


def _ffn(x, win, wout, g, b, *, tm):
    n, d = x.shape
    dff = wout.shape[0]
    chunk = FFN_CHUNK
    kern = functools.partial(_ffn_kernel, n_chunks=dff // chunk, chunk=chunk)
    return pl.pallas_call(
        kern,
        grid=(n // tm,),
        in_specs=[pl.BlockSpec((tm, d), lambda i: (i, 0)),
                  _resident(win.shape), _resident(wout.shape),
                  _resident(g.shape), _resident(b.shape)],
        out_specs=pl.BlockSpec((tm, d), lambda i: (i, 0)),
        out_shape=jax.ShapeDtypeStruct((n, d), F32),
        compiler_params=_compiler_params(1),
        name="ffn",
    )(x, win, wout, g, b)


def _conv_kernel(x_ref, xprev_ref, xnext_ref, win_ref, ck_ref, wout_ref, g_ref, b_ref,
                 o_ref, v_ref, *, tiles_per_seq):
    i = pl.program_id(0)
    tm, d = x_ref.shape
    t = i % tiles_per_seq
    x = x_ref[...]
    xp = jnp.where(t == 0, 0.0, xprev_ref[...])
    xn = jnp.where(t == tiles_per_seq - 1, 0.0, xnext_ref[...])
    xcat = jnp.concatenate([xp, x, xn], axis=0).astype(BF16)
    ch = jnp.dot(xcat, win_ref[:, d:], preferred_element_type=F32)
    v_ref[...] = ch[:, :d] * ch[:, d:]
    bgate = jnp.dot(x.astype(BF16), win_ref[:, :d], preferred_element_type=F32)
    ck = ck_ref[...]
    conv = (ck[0:1] * v_ref[SUBLANES - 1:SUBLANES - 1 + tm, :]
            + ck[1:2] * v_ref[SUBLANES:SUBLANES + tm, :]
            + ck[2:3] * v_ref[SUBLANES + 1:SUBLANES + 1 + tm, :])
    mix = jnp.dot((bgate * conv).astype(BF16), wout_ref[...], preferred_element_type=F32)
    o_ref[...] = _layer_norm(DEEPNORM_ALPHA * x + mix, g_ref[...], b_ref[...])


def _conv_mixer(x, win, ck, wout, g, b, *, seq, tm):
    n, d = x.shape
    rb = tm // SUBLANES
    last_rb = n // SUBLANES - 1
    kern = functools.partial(_conv_kernel, tiles_per_seq=seq // tm)
    return pl.pallas_call(
        kern,
        grid=(n // tm,),
        in_specs=[pl.BlockSpec((tm, d), lambda i: (i, 0)),
                  pl.BlockSpec((SUBLANES, d), lambda i: (jnp.maximum(i * rb - 1, 0), 0)),
                  pl.BlockSpec((SUBLANES, d), lambda i: (jnp.minimum((i + 1) * rb, last_rb), 0)),
                  _resident(win.shape), _resident(ck.shape), _resident(wout.shape),
                  _resident(g.shape), _resident(b.shape)],
        out_specs=pl.BlockSpec((tm, d), lambda i: (i, 0)),
        out_shape=jax.ShapeDtypeStruct((n, d), F32),
        scratch_shapes=[pltpu.VMEM((tm + 2 * SUBLANES, d), F32)],
        compiler_params=_compiler_params(1),
        name="conv_mixer",
    )(x, x, x, win, ck, wout, g, b)


def _rope_tables(seq):
    t = jnp.arange(seq, dtype=jnp.int32)
    half = ROPE_AXIS_DIM // 2
    inv_freq = ROPE_THETA ** (-jnp.arange(0, ROPE_AXIS_DIM, 2, dtype=F32) / ROPE_AXIS_DIM)
    cos_parts, sin_parts = [], []
    for pos in (t // GRID_W, t % GRID_W):
        ang = pos.astype(F32)[:, None] * inv_freq[None, :]
        assert ang.shape == (seq, half)
        cos_parts += [jnp.cos(ang), jnp.cos(ang)]
        sin_parts += [-jnp.sin(ang), jnp.sin(ang)]
    return jnp.concatenate(cos_parts, axis=-1), jnp.concatenate(sin_parts, axis=-1)


def _qkv_kernel(x_ref, w_ref, qg_ref, kg_ref, cos_ref, sin_ref, q_ref, k_ref, vt_ref):
    qkv = jnp.dot(x_ref[...].astype(BF16), w_ref[...], preferred_element_type=F32)
    cos = cos_ref[...]
    sin = sin_ref[...]
    half = ROPE_AXIS_DIM // 2
    lane = lax.broadcasted_iota(jnp.int32, cos.shape, 1)
    first_half = (lane % ROPE_AXIS_DIM) < half

    def norm_rope(seg, gain):
        ms = jnp.mean(seg * seg, axis=-1, keepdims=True)
        y = seg * lax.rsqrt(ms + QK_EPS) * gain
        partner = jnp.where(first_half,
                            pltpu.roll(y, HEAD_DIM - half, axis=1),
                            pltpu.roll(y, half, axis=1))
        return y * cos + partner * sin

    qg = qg_ref[...]
    kg = kg_ref[...]
    for h in range(N_HEADS):
        seg = qkv[:, h * HEAD_DIM:(h + 1) * HEAD_DIM]
        q_ref[:, h * HEAD_DIM:(h + 1) * HEAD_DIM] = (
            norm_rope(seg, qg) * (HEAD_DIM ** -0.5 * LOG2_E)).astype(BF16)
    for h in range(N_KV_HEADS):
        o = (N_HEADS + h) * HEAD_DIM
        k_ref[:, h * HEAD_DIM:(h + 1) * HEAD_DIM] = norm_rope(qkv[:, o:o + HEAD_DIM], kg).astype(BF16)
        o += N_KV_HEADS * HEAD_DIM
        vt_ref[h, 0] = qkv[:, o:o + HEAD_DIM].T.astype(BF16)


def _qkv(x, w, qg, kg, cos, sin, *, seq, tm):
    n, d = x.shape
    tiles_per_seq = seq // tm
    dq = N_HEADS * HEAD_DIM
    dkv = N_KV_HEADS * HEAD_DIM
    return pl.pallas_call(
        _qkv_kernel,
        grid=(n // tm,),
        in_specs=[pl.BlockSpec((tm, d), lambda i: (i, 0)),
                  _resident(w.shape), _resident(qg.shape), _resident(kg.shape),
                  pl.BlockSpec((tm, HEAD_DIM), lambda i: (i % tiles_per_seq, 0)),
                  pl.BlockSpec((tm, HEAD_DIM), lambda i: (i % tiles_per_seq, 0))],
        out_specs=[pl.BlockSpec((tm, dq), lambda i: (i, 0)),
                   pl.BlockSpec((tm, dkv), lambda i: (i, 0)),
                   pl.BlockSpec((N_KV_HEADS, 1, HEAD_DIM, tm), lambda i: (0, i, 0, 0))],
        out_shape=[jax.ShapeDtypeStruct((n, dq), BF16),
                   jax.ShapeDtypeStruct((n, dkv), BF16),
                   jax.ShapeDtypeStruct((N_KV_HEADS, n // tm, HEAD_DIM, tm), BF16)],
        compiler_params=_compiler_params(1),
        name="qkv_rope",
    )(x, w, qg, kg, cos, sin)


def _flash_kernel(q_ref, k_ref, vt_ref, o_ref, qs_ref, s_ref, acc_ref, *, tk):
    tq = q_ref.shape[0]
    pw = 2 * tq
    n_kv = k_ref.shape[0] // tk
    for h in range(GROUP):
        qs_ref[h * tq:(h + 1) * tq, :] = q_ref[:, h * HEAD_DIM:(h + 1) * HEAD_DIM]
    acc_ref[...] = jnp.zeros(acc_ref.shape, F32)

    def scores(j, pair):
        kb = k_ref[pl.ds(pl.multiple_of(j * tk, tk), tk), :]
        s = lax.dot_general(kb, qs_ref[pair * pw:(pair + 1) * pw, :], (((1,), (1,)), ((), ())),
                            preferred_element_type=F32)
        s_ref[pair] = s
        return jnp.max(s, axis=0, keepdims=True)

    def softmax_pv(j, pair, mx, m, l):
        m_new = jnp.maximum(m, mx)
        alpha = jnp.exp2(m - m_new)
        p = jnp.exp2(s_ref[pair] - m_new)
        l_new = alpha * l + jnp.sum(p, axis=0, keepdims=True)
        acc_ref[pair] = alpha * acc_ref[pair] + jnp.dot(vt_ref[0, j], p.astype(BF16),
                                                        preferred_element_type=F32)
        return m_new, l_new

    def step(j, carry):
        mx0, m0, l0, m1, l1 = carry
        mx1 = scores(j, 1)
        m0, l0 = softmax_pv(j, 0, mx0, m0, l0)
        mx0 = scores(jnp.minimum(j + 1, n_kv - 1), 0)
        m1, l1 = softmax_pv(j, 1, mx1, m1, l1)
        return mx0, m0, l0, m1, l1

    neg_inf = jnp.full((1, pw), -jnp.inf, F32)
    zero = jnp.zeros((1, pw), F32)
    _, _, l0, _, l1 = lax.fori_loop(0, n_kv, step, (scores(0, 0), neg_inf, zero, neg_inf, zero))
    for h in range(GROUP):
        pair, c = divmod(h, 2)
        l = (l0, l1)[pair][:, c * tq:(c + 1) * tq]
        out = acc_ref[pair, :, c * tq:(c + 1) * tq] / l
        o_ref[:, h * HEAD_DIM:(h + 1) * HEAD_DIM] = out.T.astype(BF16)


def _flash(q, k, vt, *, batch, seq, tq, tk):
    n = q.shape[0]
    gw = GROUP * HEAD_DIM
    q_tiles = seq // tq
    n_kv = seq // tk
    assert vt.shape == (N_KV_HEADS, n // tk, HEAD_DIM, tk)
    kern = functools.partial(_flash_kernel, tk=tk)
    return pl.pallas_call(
        kern,
        grid=(batch, N_KV_HEADS, q_tiles),
        in_specs=[pl.BlockSpec((tq, gw), lambda b, g, i: (b * q_tiles + i, g)),
                  pl.BlockSpec((seq, HEAD_DIM), lambda b, g, i: (b, g)),
                  pl.BlockSpec((1, n_kv, HEAD_DIM, tk), lambda b, g, i: (g, b, 0, 0))],
        out_specs=pl.BlockSpec((tq, gw), lambda b, g, i: (b * q_tiles + i, g)),
        out_shape=jax.ShapeDtypeStruct((n, N_HEADS * HEAD_DIM), BF16),
        scratch_shapes=[pltpu.VMEM((GROUP * tq, HEAD_DIM), BF16),
                        pltpu.VMEM((GROUP // 2, tk, 2 * tq), F32),
                        pltpu.VMEM((GROUP // 2, HEAD_DIM, 2 * tq), F32)],
        compiler_params=_compiler_params(3),
        name="flash_gqa",
    )(q, k, vt)


def _proj_ln_kernel(a_ref, x_ref, w_ref, g_ref, b_ref, o_ref):
    mix = jnp.dot(a_ref[...], w_ref[...], preferred_element_type=F32)
    o_ref[...] = _layer_norm(DEEPNORM_ALPHA * x_ref[...] + mix, g_ref[...], b_ref[...])


def _proj_ln(a, x, w, g, b, *, tm):
    n, d = x.shape
    return pl.pallas_call(
        _proj_ln_kernel,
        grid=(n // tm,),
        in_specs=[pl.BlockSpec((tm, a.shape[1]), lambda i: (i, 0)),
                  pl.BlockSpec((tm, d), lambda i: (i, 0)),
                  _resident(w.shape), _resident(g.shape), _resident(b.shape)],
        out_specs=pl.BlockSpec((tm, d), lambda i: (i, 0)),
        out_shape=jax.ShapeDtypeStruct((n, d), F32),
        compiler_params=_compiler_params(1),
        name="attn_out_ln",
    )(a, x, w, g, b)


def _pack_ffn_in(w_in):
    nl, d, two_dff = w_in.shape
    dff = two_dff // 2
    w = w_in.reshape(nl, d, 2, dff // FFN_CHUNK, FFN_CHUNK)
    return w.transpose(0, 1, 3, 2, 4).reshape(nl, d, two_dff).astype(BF16)


def _trunk(x, p):
    batch, seq, d = x.shape
    tm = min(TOKEN_TILE, seq)
    tq = min(Q_TILE, seq)
    tk = min(KV_TILE, seq)
    assert seq % tm == 0 and seq % tq == 0 and seq % GRID_W == 0
    assert tk == tm, "the qkv kernel emits v^T in kv-tile-sized blocks"
    x = x.reshape(batch * seq, d)
    cos, sin = _rope_tables(seq)
    for i in range(DEPTH):
        g = p["ln_g"][i][:, None, :]
        b = p["ln_b"][i][:, None, :]
        x = _ffn(x, p["ffn1_w_in"][i], p["ffn1_w_out"][i], g[0], b[0], tm=tm)
        j = i // 2
        if i % 2 == 0:
            x = _conv_mixer(x, p["conv_w_in"][j], p["conv_k"][j], p["conv_w_out"][j],
                            g[1], b[1], seq=seq, tm=tm)
        else:
            q, k, vt = _qkv(x, p["attn_w_qkv"][j], p["attn_q_norm"][j][None, :],
                            p["attn_k_norm"][j][None, :], cos, sin, seq=seq, tm=tm)
            o = _flash(q, k, vt, batch=batch, seq=seq, tq=tq, tk=tk)
            x = _proj_ln(o, x, p["attn_w_out"][j], g[1], b[1], tm=tm)
        x = _ffn(x, p["ffn2_w_in"][i], p["ffn2_w_out"][i], g[2], b[2], tm=tm)
    return x.reshape(batch, seq, d)


def kernel(x_prompt, x_sample, ffn1_w_in, ffn1_w_out, ffn2_w_in, ffn2_w_out, ln_g, ln_b,
           conv_w_in, conv_k, conv_w_out, attn_w_qkv, attn_q_norm, attn_k_norm, attn_w_out):
    p = {
        "ffn1_w_in": _pack_ffn_in(ffn1_w_in), "ffn1_w_out": ffn1_w_out.astype(BF16),
        "ffn2_w_in": _pack_ffn_in(ffn2_w_in), "ffn2_w_out": ffn2_w_out.astype(BF16),
        "ln_g": ln_g, "ln_b": ln_b,
        "conv_w_in": conv_w_in.astype(BF16), "conv_k": conv_k, "conv_w_out": conv_w_out.astype(BF16),
        "attn_w_qkv": attn_w_qkv.astype(BF16), "attn_q_norm": attn_q_norm,
        "attn_k_norm": attn_k_norm, "attn_w_out": attn_w_out.astype(BF16),
    }
    return (_trunk(x_prompt, p), _trunk(x_sample, p))
```

```python
import functools

import jax
import jax.numpy as jnp
import numpy as np
from jax import lax
from jax.experimental import pallas as pl
from jax.experimental.pallas import tpu as pltpu

DEPTH = 4
HEAD_DIM = 128
N_KV_HEADS = 2
GROUP = 4
N_HEADS = N_KV_HEADS * GROUP
ROPE_AXIS_DIM = HEAD_DIM // 2
ROPE_THETA = 10000.0
GRID_W = 64
CONV_WIDTH = 3
LN_EPS = 1e-5
QK_EPS = 1e-6
DEEPNORM_ALPHA = (2.0 * DEPTH) ** 0.25
LOG2_E = 1.4426950408889634

LANES = 128
SUBLANES = 8
MXU_DIM = 256
VMEM_LIMIT_BYTES = 56 * 1024 * 1024

TOKEN_TILE = 512
FFN_CHUNK = MXU_DIM
Q_TILE = 256
KV_TILE = 512

F32 = jnp.float32
BF16 = jnp.bfloat16


def _compiler_params(n_grid):
    return pltpu.CompilerParams(dimension_semantics=("arbitrary",) * n_grid,
                                vmem_limit_bytes=VMEM_LIMIT_BYTES)


def _layer_norm(y, g, b):
    mu = jnp.mean(y, axis=-1, keepdims=True)
    yc = y - mu
    var = jnp.mean(yc * yc, axis=-1, keepdims=True)
    return yc * lax.rsqrt(var + LN_EPS) * g + b


def _resident(shape):
    return pl.BlockSpec(shape, lambda *_: (0,) * len(shape))


def _ffn_kernel(x_ref, win_ref, wout_ref, g_ref, b_ref, o_ref, *, n_chunks, chunk):
    x = x_ref[...]
    xb = x.astype(BF16)
    acc = jnp.zeros(x.shape, F32)
    for c in range(n_chunks):
        gu = jnp.dot(xb, win_ref[:, c * 2 * chunk:(c + 1) * 2 * chunk],
                     preferred_element_type=F32)
        gate = gu[:, :chunk]
        up = gu[:, chunk:]
        h = gate / (1.0 + jnp.exp(-gate)) * up
        acc = acc + jnp.dot(h.astype(BF16), wout_ref[c * chunk:(c + 1) * chunk, :],
                            preferred_element_type=F32)
    y = DEEPNORM_ALPHA * x + 0.5 * acc
    o_ref[...] = _layer_norm(y, g_ref[...], b_ref[...])


def _ffn(x, win, wout, g, b, *, tm):
    n, d = x.shape
    dff = wout.shape[0]
    chunk = FFN_CHUNK
    kern = functools.partial(_ffn_kernel, n_chunks=dff // chunk, chunk=chunk)
    return pl.pallas_call(
        kern,
        grid=(n // tm,),
        in_specs=[pl.BlockSpec((tm, d), lambda i: (i, 0)),
                  _resident(win.shape), _resident(wout.shape),
                  _resident(g.shape), _resident(b.shape)],
        out_specs=pl.BlockSpec((tm, d), lambda i: (i, 0)),
        out_shape=jax.ShapeDtypeStruct((n, d), F32),
        compiler_params=_compiler_params(1),
        name="ffn",
    )(x, win, wout, g, b)


def _conv_kernel(x_ref, xprev_ref, xnext_ref, win_ref, ck_ref, wout_ref, g_ref, b_ref,
                 o_ref, v_ref, *, tiles_per_seq):
    i = pl.program_id(0)
    tm, d = x_ref.shape
    t = i % tiles_per_seq
    x = x_ref[...]
    xp = jnp.where(t == 0, 0.0, xprev_ref[...])
    xn = jnp.where(t == tiles_per_seq - 1, 0.0, xnext_ref[...])
    xcat = jnp.concatenate([xp, x, xn], axis=0).astype(BF16)
    ch = jnp.dot(xcat, win_ref[:, d:], preferred_element_type=F32)
    v_ref[...] = ch[:, :d] * ch[:, d:]
    bgate = jnp.dot(x.astype(BF16), win_ref[:, :d], preferred_element_type=F32)
    ck = ck_ref[...]
    conv = (ck[0:1] * v_ref[SUBLANES - 1:SUBLANES - 1 + tm, :]
            + ck[1:2] * v_ref[SUBLANES:SUBLANES + tm, :]
            + ck[2:3] * v_ref[SUBLANES + 1:SUBLANES + 1 + tm, :])
    mix = jnp.dot((bgate * conv).astype(BF16), wout_ref[...], preferred_element_type=F32)
    o_ref[...] = _layer_norm(DEEPNORM_ALPHA * x + mix, g_ref[...], b_ref[...])


def _conv_mixer(x, win, ck, wout, g, b, *, seq, tm):
    n, d = x.shape
    rb = tm // SUBLANES
    last_rb = n // SUBLANES - 1
    kern = functools.partial(_conv_kernel, tiles_per_seq=seq // tm)
    return pl.pallas_call(
        kern,
        grid=(n // tm,),
        in_specs=[pl.BlockSpec((tm, d), lambda i: (i, 0)),
                  pl.BlockSpec((SUBLANES, d), lambda i: (jnp.maximum(i * rb - 1, 0), 0)),
                  pl.BlockSpec((SUBLANES, d), lambda i: (jnp.minimum((i + 1) * rb, last_rb), 0)),
                  _resident(win.shape), _resident(ck.shape), _resident(wout.shape),
                  _resident(g.shape), _resident(b.shape)],
        out_specs=pl.BlockSpec((tm, d), lambda i: (i, 0)),
        out_shape=jax.ShapeDtypeStruct((n, d), F32),
        scratch_shapes=[pltpu.VMEM((tm + 2 * SUBLANES, d), F32)],
        compiler_params=_compiler_params(1),
        name="conv_mixer",
    )(x, x, x, win, ck, wout, g, b)


def _rope_tables(seq):
    t = jnp.arange(seq, dtype=jnp.int32)
    half = ROPE_AXIS_DIM // 2
    inv_freq = ROPE_THETA ** (-jnp.arange(0, ROPE_AXIS_DIM, 2, dtype=F32) / ROPE_AXIS_DIM)
    cos_parts, sin_parts = [], []
    for pos in (t // GRID_W, t % GRID_W):
        ang = pos.astype(F32)[:, None] * inv_freq[None, :]
        assert ang.shape == (seq, half)
        cos_parts += [jnp.cos(ang), jnp.cos(ang)]
        sin_parts += [-jnp.sin(ang), jnp.sin(ang)]
    return jnp.concatenate(cos_parts, axis=-1), jnp.concatenate(sin_parts, axis=-1)


def _qkv_kernel(x_ref, w_ref, qg_ref, kg_ref, cos_ref, sin_ref, q_ref, k_ref, vt_ref):
    qkv = jnp.dot(x_ref[...].astype(BF16), w_ref[...], preferred_element_type=F32)
    cos = cos_ref[...]
    sin = sin_ref[...]
    half = ROPE_AXIS_DIM // 2
    lane = lax.broadcasted_iota(jnp.int32, cos.shape, 1)
    first_half = (lane % ROPE_AXIS_DIM) < half

    def norm_rope(seg, gain):
        ms = jnp.mean(seg * seg, axis=-1, keepdims=True)
        y = seg * lax.rsqrt(ms + QK_EPS) * gain
        partner = jnp.where(first_half,
                            pltpu.roll(y, HEAD_DIM - half, axis=1),
                            pltpu.roll(y, half, axis=1))
        return y * cos + partner * sin

    qg = qg_ref[...]
    kg = kg_ref[...]
    for h in range(N_HEADS):
        seg = qkv[:, h * HEAD_DIM:(h + 1) * HEAD_DIM]
        q_ref[:, h * HEAD_DIM:(h + 1) * HEAD_DIM] = (
            norm_rope(seg, qg) * (HEAD_DIM ** -0.5 * LOG2_E)).astype(BF16)
    for h in range(N_KV_HEADS):
        o = (N_HEADS + h) * HEAD_DIM
        k_ref[:, h * HEAD_DIM:(h + 1) * HEAD_DIM] = norm_rope(qkv[:, o:o + HEAD_DIM], kg).astype(BF16)
        o += N_KV_HEADS * HEAD_DIM
        vt_ref[h, 0] = qkv[:, o:o + HEAD_DIM].T.astype(BF16)


def _qkv(x, w, qg, kg, cos, sin, *, seq, tm):
    n, d = x.shape
    tiles_per_seq = seq // tm
    dq = N_HEADS * HEAD_DIM
    dkv = N_KV_HEADS * HEAD_DIM
    return pl.pallas_call(
        _qkv_kernel,
        grid=(n // tm,),
        in_specs=[pl.BlockSpec((tm, d), lambda i: (i, 0)),
                  _resident(w.shape), _resident(qg.shape), _resident(kg.shape),
                  pl.BlockSpec((tm, HEAD_DIM), lambda i: (i % tiles_per_seq, 0)),
                  pl.BlockSpec((tm, HEAD_DIM), lambda i: (i % tiles_per_seq, 0))],
        out_specs=[pl.BlockSpec((tm, dq), lambda i: (i, 0)),
                   pl.BlockSpec((tm, dkv), lambda i: (i, 0)),
                   pl.BlockSpec((N_KV_HEADS, 1, HEAD_DIM, tm), lambda i: (0, i, 0, 0))],
        out_shape=[jax.ShapeDtypeStruct((n, dq), BF16),
                   jax.ShapeDtypeStruct((n, dkv), BF16),
                   jax.ShapeDtypeStruct((N_KV_HEADS, n // tm, HEAD_DIM, tm), BF16)],
        compiler_params=_compiler_params(1),
        name="qkv_rope",
    )(x, w, qg, kg, cos, sin)


def _flash_kernel(q_ref, k_ref, vt_ref, o_ref, qt_ref, s_ref, acc_ref, *, tk):
    tq = q_ref.shape[0]
    pw = 2 * tq
    n_kv = k_ref.shape[0] // tk
    for h in range(GROUP):
        qh = q_ref[:, h * HEAD_DIM:(h + 1) * HEAD_DIM].astype(F32)
        qt_ref[:, h * tq:(h + 1) * tq] = qh.T.astype(BF16)
    acc_ref[...] = jnp.zeros(acc_ref.shape, F32)

    def scores(j, pair):
        kb = k_ref[pl.ds(pl.multiple_of(j * tk, tk), tk), :]
        s = jnp.dot(kb, qt_ref[:, pair * pw:(pair + 1) * pw], preferred_element_type=F32)
        s_ref[pair] = s
        return jnp.max(s, axis=0, keepdims=True)

    def softmax_pv(j, pair, mx, m, l):
        m_new = jnp.maximum(m, mx)
        alpha = jnp.exp2(m - m_new)
        p = jnp.exp2(s_ref[pair] - m_new)
        l_new = alpha * l + jnp.sum(p, axis=0, keepdims=True)
        acc_ref[pair] = alpha * acc_ref[pair] + jnp.dot(vt_ref[0, j], p.astype(BF16),
                                                        preferred_element_type=F32)
        return m_new, l_new

    def step(j, carry):
        mx0, m0, l0, m1, l1 = carry
        mx1 = scores(j, 1)
        m0, l0 = softmax_pv(j, 0, mx0, m0, l0)
        mx0 = scores(jnp.minimum(j + 1, n_kv - 1), 0)
        m1, l1 = softmax_pv(j, 1, mx1, m1, l1)
        return mx0, m0, l0, m1, l1

    neg_inf = jnp.full((1, pw), -jnp.inf, F32)
    zero = jnp.zeros((1, pw), F32)
    _, _, l0, _, l1 = lax.fori_loop(0, n_kv, step, (scores(0, 0), neg_inf, zero, neg_inf, zero))
    for h in range(GROUP):
        pair, c = divmod(h, 2)
        l = (l0, l1)[pair][:, c * tq:(c + 1) * tq]
        out = acc_ref[pair, :, c * tq:(c + 1) * tq] / l
        o_ref[:, h * HEAD_DIM:(h + 1) * HEAD_DIM] = out.T.astype(BF16)


def _flash(q, k, vt, *, batch, seq, tq, tk):
    n = q.shape[0]
    gw = GROUP * HEAD_DIM
    q_tiles = seq // tq
    n_kv = seq // tk
    assert vt.shape == (N_KV_HEADS, n // tk, HEAD_DIM, tk)
    kern = functools.partial(_flash_kernel, tk=tk)
    return pl.pallas_call(
        kern,
        grid=(batch, N_KV_HEADS, q_tiles),
        in_specs=[pl.BlockSpec((tq, gw), lambda b, g, i: (b * q_tiles + i, g)),
                  pl.BlockSpec((seq, HEAD_DIM), lambda b, g, i: (b, g)),
                  pl.BlockSpec((1, n_kv, HEAD_DIM, tk), lambda b, g, i: (g, b, 0, 0))],
        out_specs=pl.BlockSpec((tq, gw), lambda b, g, i: (b * q_tiles + i, g)),
        out_shape=jax.ShapeDtypeStruct((n, N_HEADS * HEAD_DIM), BF16),
        scratch_shapes=[pltpu.VMEM((HEAD_DIM, GROUP * tq), BF16),
                        pltpu.VMEM((GROUP // 2, tk, 2 * tq), F32),
                        pltpu.VMEM((GROUP // 2, HEAD_DIM, 2 * tq), F32)],
        compiler_params=_compiler_params(3),
        name="flash_gqa",
    )(q, k, vt)


def _proj_ln_kernel(a_ref, x_ref, w_ref, g_ref, b_ref, o_ref):
    mix = jnp.dot(a_ref[...], w_ref[...], preferred_element_type=F32)
    o_ref[...] = _layer_norm(DEEPNORM_ALPHA * x_ref[...] + mix, g_ref[...], b_ref[...])


def _proj_ln(a, x, w, g, b, *, tm):
    n, d = x.shape
    return pl.pallas_call(
        _proj_ln_kernel,
        grid=(n // tm,),
        in_specs=[pl.BlockSpec((tm, a.shape[1]), lambda i: (i, 0)),
                  pl.BlockSpec((tm, d), lambda i: (i, 0)),
                  _resident(w.shape), _resident(g.shape), _resident(b.shape)],
        out_specs=pl.BlockSpec((tm, d), lambda i: (i, 0)),
        out_shape=jax.ShapeDtypeStruct((n, d), F32),
        compiler_params=_compiler_params(1),
        name="attn_out_ln",
    )(a, x, w, g, b)


def _pack_ffn_in(w_in):
    nl, d, two_dff = w_in.shape
    dff = two_dff // 2
    w = w_in.reshape(nl, d, 2, dff // FFN_CHUNK, FFN_CHUNK)
    return w.transpose(0, 1, 3, 2, 4).reshape(nl, d, two_dff).astype(BF16)


def _trunk(x, p):
    batch, seq, d = x.shape
    tm = min(TOKEN_TILE, seq)
    tq = min(Q_TILE, seq)
    tk = min(KV_TILE, seq)
    assert seq % tm == 0 and seq % tq == 0 and seq % GRID_W == 0
    assert tk == tm, "the qkv kernel emits v^T in kv-tile-sized blocks"
    x = x.reshape(batch * seq, d)
    cos, sin = _rope_tables(seq)
    for i in range(DEPTH):
        g = p["ln_g"][i][:, None, :]
        b = p["ln_b"][i][:, None, :]
        x = _ffn(x, p["ffn1_w_in"][i], p["ffn1_w_out"][i], g[0], b[0], tm=tm)
        j = i // 2
        if i % 2 == 0:
            x = _conv_mixer(x, p["conv_w_in"][j], p["conv_k"][j], p["conv_w_out"][j],
                            g[1], b[1], seq=seq, tm=tm)
        else:
            q, k, vt = _qkv(x, p["attn_w_qkv"][j], p["attn_q_norm"][j][None, :],
                            p["attn_k_norm"][j][None, :], cos, sin, seq=seq, tm=tm)
            o = _flash(q, k, vt, batch=batch, seq=seq, tq=tq, tk=tk)
            x = _proj_ln(o, x, p["attn_w_out"][j], g[1], b[1], tm=tm)
        x = _ffn(x, p["ffn2_w_in"][i], p["ffn2_w_out"][i], g[2], b[2], tm=tm)
    return x.reshape(batch, seq, d)


def kernel(x_prompt, x_sample, ffn1_w_in, ffn1_w_out, ffn2_w_in, ffn2_w_out, ln_g, ln_b,
           conv_w_in, conv_k, conv_w_out, attn_w_qkv, attn_q_norm, attn_k_norm, attn_w_out):
    p = {
        "ffn1_w_in": _pack_ffn_in(ffn1_w_in), "ffn1_w_out": ffn1_w_out.astype(BF16),
        "ffn2_w_in": _pack_ffn_in(ffn2_w_in), "ffn2_w_out": ffn2_w_out.astype(BF16),
        "ln_g": ln_g, "ln_b": ln_b,
        "conv_w_in": conv_w_in.astype(BF16), "conv_k": conv_k, "conv_w_out": conv_w_out.astype(BF16),
        "attn_w_qkv": attn_w_qkv.astype(BF16), "attn_q_norm": attn_q_norm,
        "attn_k_norm": attn_k_norm, "attn_w_out": attn_w_out.astype(BF16),
    }
    return (_trunk(x_prompt, p), _trunk(x_sample, p))
```

```python
import functools

import jax
import jax.numpy as jnp
import numpy as np
from jax import lax
from jax.experimental import pallas as pl
from jax.experimental.pallas import tpu as pltpu

DEPTH = 4
HEAD_DIM = 128
N_KV_HEADS = 2
GROUP = 4
N_HEADS = N_KV_HEADS * GROUP
ROPE_AXIS_DIM = HEAD_DIM // 2
ROPE_THETA = 10000.0
GRID_W = 64
CONV_WIDTH = 3
LN_EPS = 1e-5
QK_EPS = 1e-6
DEEPNORM_ALPHA = (2.0 * DEPTH) ** 0.25
LOG2_E = 1.4426950408889634

LANES = 128
SUBLANES = 8
MXU_DIM = 256
VMEM_LIMIT_BYTES = 56 * 1024 * 1024

TOKEN_TILE = 512
FFN_CHUNK = MXU_DIM
Q_TILE = 256
KV_TILE = 512
KV_UNROLL = 4

F32 = jnp.float32
BF16 = jnp.bfloat16


def _compiler_params(n_grid):
    return pltpu.CompilerParams(dimension_semantics=("arbitrary",) * n_grid,
                                vmem_limit_bytes=VMEM_LIMIT_BYTES)


def _layer_norm(y, g, b):
    mu = jnp.mean(y, axis=-1, keepdims=True)
    yc = y - mu
    var = jnp.mean(yc * yc, axis=-1, keepdims=True)
    return yc * lax.rsqrt(var + LN_EPS) * g + b


def _resident(shape):
    return pl.BlockSpec(shape, lambda *_: (0,) * len(shape))


def _ffn_kernel(x_ref, win_ref, wout_ref, g_ref, b_ref, o_ref, *, n_chunks, chunk):
    x = x_ref[...]
    xb = x.astype(BF16)
    acc = jnp.zeros(x.shape, F32)
    for c in range(n_chunks):
        gu = jnp.dot(xb, win_ref[:, c * 2 * chunk:(c + 1) * 2 * chunk],
                     preferred_element_type=F32)
        gate = gu[:, :chunk]
        up = gu[:, chunk:]
        h = gate / (1.0 + jnp.exp(-gate)) * up
        acc = acc + jnp.dot(h.astype(BF16), wout_ref[c * chunk:(c + 1) * chunk, :],
                            preferred_element_type=F32)
    y = DEEPNORM_ALPHA * x + 0.5 * acc
    o_ref[...] = _layer_norm(y, g_ref[...], b_ref[...])


def _ffn(x, win, wout, g, b, *, tm):
    n, d = x.shape
    dff = wout.shape[0]
    chunk = FFN_CHUNK
    kern = functools.partial(_ffn_kernel, n_chunks=dff // chunk, chunk=chunk)
    return pl.pallas_call(
        kern,
        grid=(n // tm,),
        in_specs=[pl.BlockSpec((tm, d), lambda i: (i, 0)),
                  _resident(win.shape), _resident(wout.shape),
                  _resident(g.shape), _resident(b.shape)],
        out_specs=pl.BlockSpec((tm, d), lambda i: (i, 0)),
        out_shape=jax.ShapeDtypeStruct((n, d), F32),
        compiler_params=_compiler_params(1),
        name="ffn",
    )(x, win, wout, g, b)


def _conv_kernel(x_ref, xprev_ref, xnext_ref, win_ref, ck_ref, wout_ref, g_ref, b_ref,
                 o_ref, v_ref, *, tiles_per_seq):
    i = pl.program_id(0)
    tm, d = x_ref.shape
    t = i % tiles_per_seq
    x = x_ref[...]
    xp = jnp.where(t == 0, 0.0, xprev_ref[...])
    xn = jnp.where(t == tiles_per_seq - 1, 0.0, xnext_ref[...])
    xcat = jnp.concatenate([xp, x, xn], axis=0).astype(BF16)
    ch = jnp.dot(xcat, win_ref[:, d:], preferred_element_type=F32)
    v_ref[...] = ch[:, :d] * ch[:, d:]
    bgate = jnp.dot(x.astype(BF16), win_ref[:, :d], preferred_element_type=F32)
    ck = ck_ref[...]
    conv = (ck[0:1] * v_ref[SUBLANES - 1:SUBLANES - 1 + tm, :]
            + ck[1:2] * v_ref[SUBLANES:SUBLANES + tm, :]
            + ck[2:3] * v_ref[SUBLANES + 1:SUBLANES + 1 + tm, :])
    mix = jnp.dot((bgate * conv).astype(BF16), wout_ref[...], preferred_element_type=F32)
    o_ref[...] = _layer_norm(DEEPNORM_ALPHA * x + mix, g_ref[...], b_ref[...])


def _conv_mixer(x, win, ck, wout, g, b, *, seq, tm):
    n, d = x.shape
    rb = tm // SUBLANES
    last_rb = n // SUBLANES - 1
    kern = functools.partial(_conv_kernel, tiles_per_seq=seq // tm)
    return pl.pallas_call(
        kern,
        grid=(n // tm,),
        in_specs=[pl.BlockSpec((tm, d), lambda i: (i, 0)),
                  pl.BlockSpec((SUBLANES, d), lambda i: (jnp.maximum(i * rb - 1, 0), 0)),
                  pl.BlockSpec((SUBLANES, d), lambda i: (jnp.minimum((i + 1) * rb, last_rb), 0)),
                  _resident(win.shape), _resident(ck.shape), _resident(wout.shape),
                  _resident(g.shape), _resident(b.shape)],
        out_specs=pl.BlockSpec((tm, d), lambda i: (i, 0)),
        out_shape=jax.ShapeDtypeStruct((n, d), F32),
        scratch_shapes=[pltpu.VMEM((tm + 2 * SUBLANES, d), F32)],
        compiler_params=_compiler_params(1),
        name="conv_mixer",
    )(x, x, x, win, ck, wout, g, b)


def _rope_tables(seq):
    t = jnp.arange(seq, dtype=jnp.int32)
    half = ROPE_AXIS_DIM // 2
    inv_freq = ROPE_THETA ** (-jnp.arange(0, ROPE_AXIS_DIM, 2, dtype=F32) / ROPE_AXIS_DIM)
    cos_parts, sin_parts = [], []
    for pos in (t // GRID_W, t % GRID_W):
        ang = pos.astype(F32)[:, None] * inv_freq[None, :]
        assert ang.shape == (seq, half)
        cos_parts += [jnp.cos(ang), jnp.cos(ang)]
        sin_parts += [-jnp.sin(ang), jnp.sin(ang)]
    return jnp.concatenate(cos_parts, axis=-1), jnp.concatenate(sin_parts, axis=-1)


def _qkv_kernel(x_ref, w_ref, qg_ref, kg_ref, cos_ref, sin_ref, q_ref, k_ref, vt_ref):
    qkv = jnp.dot(x_ref[...].astype(BF16), w_ref[...], preferred_element_type=F32)
    cos = cos_ref[...]
    sin = sin_ref[...]
    half = ROPE_AXIS_DIM // 2
    lane = lax.broadcasted_iota(jnp.int32, cos.shape, 1)
    first_half = (lane % ROPE_AXIS_DIM) < half

    def norm_rope(seg, gain):
        ms = jnp.mean(seg * seg, axis=-1, keepdims=True)
        y = seg * lax.rsqrt(ms + QK_EPS) * gain
        partner = jnp.where(first_half,
                            pltpu.roll(y, HEAD_DIM - half, axis=1),
                            pltpu.roll(y, half, axis=1))
        return y * cos + partner * sin

    qg = qg_ref[...]
    kg = kg_ref[...]
    for h in range(N_HEADS):
        seg = qkv[:, h * HEAD_DIM:(h + 1) * HEAD_DIM]
        q_ref[:, h * HEAD_DIM:(h + 1) * HEAD_DIM] = (
            norm_rope(seg, qg) * (HEAD_DIM ** -0.5 * LOG2_E)).astype(BF16)
    for h in range(N_KV_HEADS):
        o = (N_HEADS + h) * HEAD_DIM
        k_ref[:, h * HEAD_DIM:(h + 1) * HEAD_DIM] = norm_rope(qkv[:, o:o + HEAD_DIM], kg).astype(BF16)
        o += N_KV_HEADS * HEAD_DIM
        vt_ref[h, 0] = qkv[:, o:o + HEAD_DIM].T.astype(BF16)


def _qkv(x, w, qg, kg, cos, sin, *, seq, tm):
    n, d = x.shape
    tiles_per_seq = seq // tm
    dq = N_HEADS * HEAD_DIM
    dkv = N_KV_HEADS * HEAD_DIM
    return pl.pallas_call(
        _qkv_kernel,
        grid=(n // tm,),
        in_specs=[pl.BlockSpec((tm, d), lambda i: (i, 0)),
                  _resident(w.shape), _resident(qg.shape), _resident(kg.shape),
                  pl.BlockSpec((tm, HEAD_DIM), lambda i: (i % tiles_per_seq, 0)),
                  pl.BlockSpec((tm, HEAD_DIM), lambda i: (i % tiles_per_seq, 0))],
        out_specs=[pl.BlockSpec((tm, dq), lambda i: (i, 0)),
                   pl.BlockSpec((tm, dkv), lambda i: (i, 0)),
                   pl.BlockSpec((N_KV_HEADS, 1, HEAD_DIM, tm), lambda i: (0, i, 0, 0))],
        out_shape=[jax.ShapeDtypeStruct((n, dq), BF16),
                   jax.ShapeDtypeStruct((n, dkv), BF16),
                   jax.ShapeDtypeStruct((N_KV_HEADS, n // tm, HEAD_DIM, tm), BF16)],
        compiler_params=_compiler_params(1),
        name="qkv_rope",
    )(x, w, qg, kg, cos, sin)


def _flash_kernel(q_ref, k_ref, vt_ref, o_ref, qt_ref, s_ref, acc_ref, *, tk):
    tq = q_ref.shape[0]
    pw = 2 * tq
    n_kv = k_ref.shape[0] // tk
    for h in range(GROUP):
        qh = q_ref[:, h * HEAD_DIM:(h + 1) * HEAD_DIM].astype(F32)
        qt_ref[:, h * tq:(h + 1) * tq] = qh.T.astype(BF16)
    acc_ref[...] = jnp.zeros(acc_ref.shape, F32)

    def scores(j, pair):
        kb = k_ref[pl.ds(pl.multiple_of(j * tk, tk), tk), :]
        s = jnp.dot(kb, qt_ref[:, pair * pw:(pair + 1) * pw], preferred_element_type=F32)
        s_ref[pair] = s
        return jnp.max(s, axis=0, keepdims=True)

    def softmax_pv(j, pair, mx, m, l):
        m_new = jnp.maximum(m, mx)
        alpha = jnp.exp2(m - m_new)
        p = jnp.exp2(s_ref[pair] - m_new)
        l_new = alpha * l + jnp.sum(p, axis=0, keepdims=True)
        acc_ref[pair] = alpha * acc_ref[pair] + jnp.dot(vt_ref[0, j], p.astype(BF16),
                                                        preferred_element_type=F32)
        return m_new, l_new

    def step(j, carry):
        mx0, m0, l0, m1, l1 = carry
        mx1 = scores(j, 1)
        m0, l0 = softmax_pv(j, 0, mx0, m0, l0)
        mx0 = scores(jnp.minimum(j + 1, n_kv - 1), 0)
        m1, l1 = softmax_pv(j, 1, mx1, m1, l1)
        return mx0, m0, l0, m1, l1

    neg_inf = jnp.full((1, pw), -jnp.inf, F32)
    zero = jnp.zeros((1, pw), F32)
    _, _, l0, _, l1 = lax.fori_loop(0, n_kv, step, (scores(0, 0), neg_inf, zero, neg_inf, zero),
                                    unroll=KV_UNROLL)
    for h in range(GROUP):
        pair, c = divmod(h, 2)
        l = (l0, l1)[pair][:, c * tq:(c + 1) * tq]
        out = acc_ref[pair, :, c * tq:(c + 1) * tq] / l
        o_ref[:, h * HEAD_DIM:(h + 1) * HEAD_DIM] = out.T.astype(BF16)


def _flash(q, k, vt, *, batch, seq, tq, tk):
    n = q.shape[0]
    gw = GROUP * HEAD_DIM
    q_tiles = seq // tq
    n_kv = seq // tk
    assert vt.shape == (N_KV_HEADS, n // tk, HEAD_DIM, tk)
    kern = functools.partial(_flash_kernel, tk=tk)
    return pl.pallas_call(
        kern,
        grid=(batch, N_KV_HEADS, q_tiles),
        in_specs=[pl.BlockSpec((tq, gw), lambda b, g, i: (b * q_tiles + i, g)),
                  pl.BlockSpec((seq, HEAD_DIM), lambda b, g, i: (b, g)),
                  pl.BlockSpec((1, n_kv, HEAD_DIM, tk), lambda b, g, i: (g, b, 0, 0))],
        out_specs=pl.BlockSpec((tq, gw), lambda b, g, i: (b * q_tiles + i, g)),
        out_shape=jax.ShapeDtypeStruct((n, N_HEADS * HEAD_DIM), BF16),
        scratch_shapes=[pltpu.VMEM((HEAD_DIM, GROUP * tq), BF16),
                        pltpu.VMEM((GROUP // 2, tk, 2 * tq), F32),
                        pltpu.VMEM((GROUP // 2, HEAD_DIM, 2 * tq), F32)],
        compiler_params=_compiler_params(3),
        name="flash_gqa",
    )(q, k, vt)


def _proj_ln_kernel(a_ref, x_ref, w_ref, g_ref, b_ref, o_ref):
    mix = jnp.dot(a_ref[...], w_ref[...], preferred_element_type=F32)
    o_ref[...] = _layer_norm(DEEPNORM_ALPHA * x_ref[...] + mix, g_ref[...], b_ref[...])


def _proj_ln(a, x, w, g, b, *, tm):
    n, d = x.shape
    return pl.pallas_call(
        _proj_ln_kernel,
        grid=(n // tm,),
        in_specs=[pl.BlockSpec((tm, a.shape[1]), lambda i: (i, 0)),
                  pl.BlockSpec((tm, d), lambda i: (i, 0)),
                  _resident(w.shape), _resident(g.shape), _resident(b.shape)],
        out_specs=pl.BlockSpec((tm, d), lambda i: (i, 0)),
        out_shape=jax.ShapeDtypeStruct((n, d), F32),
        compiler_params=_compiler_params(1),
        name="attn_out_ln",
    )(a, x, w, g, b)


def _pack_ffn_in(w_in):
    nl, d, two_dff = w_in.shape
    dff = two_dff // 2
    w = w_in.reshape(nl, d, 2, dff // FFN_CHUNK, FFN_CHUNK)
    return w.transpose(0, 1, 3, 2, 4).reshape(nl, d, two_dff).astype(BF16)


def _trunk(x, p):
    batch, seq, d = x.shape
    tm = min(TOKEN_TILE, seq)
    tq = min(Q_TILE, seq)
    tk = min(KV_TILE, seq)
    assert seq % tm == 0 and seq % tq == 0 and seq % GRID_W == 0
    assert tk == tm, "the qkv kernel emits v^T in kv-tile-sized blocks"
    x = x.reshape(batch * seq, d)
    cos, sin = _rope_tables(seq)
    for i in range(DEPTH):
        g = p["ln_g"][i][:, None, :]
        b = p["ln_b"][i][:, None, :]
        x = _ffn(x, p["ffn1_w_in"][i], p["ffn1_w_out"][i], g[0], b[0], tm=tm)
        j = i // 2
        if i % 2 == 0:
            x = _conv_mixer(x, p["conv_w_in"][j], p["conv_k"][j], p["conv_w_out"][j],
                            g[1], b[1], seq=seq, tm=tm)
        else:
            q, k, vt = _qkv(x, p["attn_w_qkv"][j], p["attn_q_norm"][j][None, :],
                            p["attn_k_norm"][j][None, :], cos, sin, seq=seq, tm=tm)
            o = _flash(q, k, vt, batch=batch, seq=seq, tq=tq, tk=tk)
            x = _proj_ln(o, x, p["attn_w_out"][j], g[1], b[1], tm=tm)
        x = _ffn(x, p["ffn2_w_in"][i], p["ffn2_w_out"][i], g[2], b[2], tm=tm)
    return x.reshape(batch, seq, d)


def kernel(x_prompt, x_sample, ffn1_w_in, ffn1_w_out, ffn2_w_in, ffn2_w_out, ln_g, ln_b,
           conv_w_in, conv_k, conv_w_out, attn_w_qkv, attn_q_norm, attn_k_norm, attn_w_out):
    p = {
        "ffn1_w_in": _pack_ffn_in(ffn1_w_in), "ffn1_w_out": ffn1_w_out.astype(BF16),
        "ffn2_w_in": _pack_ffn_in(ffn2_w_in), "ffn2_w_out": ffn2_w_out.astype(BF16),
        "ln_g": ln_g, "ln_b": ln_b,
        "conv_w_in": conv_w_in.astype(BF16), "conv_k": conv_k, "conv_w_out": conv_w_out.astype(BF16),
        "attn_w_qkv": attn_w_qkv.astype(BF16), "attn_q_norm": attn_q_norm,
        "attn_k_norm": attn_k_norm, "attn_w_out": attn_w_out.astype(BF16),
    }
    return (_trunk(x_prompt, p), _trunk(x_sample, p))
```

```python
import functools

import jax
import jax.numpy as jnp
from jax import lax
from jax.experimental import pallas as pl
from jax.experimental.pallas import tpu as pltpu

DEPTH = 4
HEAD_DIM = 128
N_KV_HEADS = 2
GROUP = 4
N_HEADS = N_KV_HEADS * GROUP
ROPE_AXIS_DIM = HEAD_DIM // 2
ROPE_THETA = 10000.0
GRID_W = 64
CONV_WIDTH = 3
N_LN = 3
LN_EPS = 1e-5
QK_EPS = 1e-6
DEEPNORM_ALPHA = (2.0 * DEPTH) ** 0.25
LOG2_E = 1.4426950408889634

LANES = 128
SUBLANES = 8
MXU_DIM = 256
VMEM_LIMIT_BYTES = 56 * 1024 * 1024

TOKEN_TILE = 1024
FFN_SUB = 512
PROJ_SUB = 256
CONV_TILE = 512
FFN_CHUNK = 2 * MXU_DIM
Q_TILE = 256
KV_TILE = 512
KV_UNROLL = 4

F32 = jnp.float32
BF16 = jnp.bfloat16


def _compiler_params(n_grid):
    return pltpu.CompilerParams(dimension_semantics=("arbitrary",) * n_grid,
                                vmem_limit_bytes=VMEM_LIMIT_BYTES)


def _layer_norm(y, g, b):
    mu = jnp.mean(y, axis=-1, keepdims=True)
    yc = y - mu
    var = jnp.mean(yc * yc, axis=-1, keepdims=True)
    return yc * lax.rsqrt(var + LN_EPS) * g + b


def _layer_spec(rows, cols, layer, col_block=0):
    return pl.BlockSpec((None, rows, cols), lambda *_: (layer, 0, col_block),
                        pipeline_mode=pl.Buffered(1))


def _pipelined_rows(n_sub, matmul_stage, vector_stage):
    pending = matmul_stage(0)
    for r in range(n_sub):
        nxt = matmul_stage(r + 1) if r + 1 < n_sub else None
        vector_stage(r, pending)
        pending = nxt


def _ffn_kernel(x_ref, wg_ref, wu_ref, wout_ref, g_ref, b_ref, o_ref, *, sub, chunks):
    gamma = g_ref[...]
    beta = b_ref[...]

    def swiglu(r):
        xb = x_ref[r * sub:(r + 1) * sub, :].astype(BF16)
        acc = None
        for off, width in chunks:
            gate = jnp.dot(xb, wg_ref[:, off:off + width], preferred_element_type=F32)
            up = jnp.dot(xb, wu_ref[:, off:off + width], preferred_element_type=F32)
            h = (gate / (1.0 + jnp.exp(-gate)) * up).astype(BF16)
            part = jnp.dot(h, wout_ref[off:off + width, :], preferred_element_type=F32)
            acc = part if acc is None else acc + part
        return acc

    def residual_ln(r, acc):
        rows = slice(r * sub, (r + 1) * sub)
        o_ref[rows, :] = _layer_norm(DEEPNORM_ALPHA * x_ref[rows, :] + 0.5 * acc, gamma, beta)

    _pipelined_rows(x_ref.shape[0] // sub, swiglu, residual_ln)


def _ffn(x, w_in, w_out, ln_g, ln_b, *, layer, ln_idx, tm):
    n, d = x.shape
    dff = w_out.shape[1]
    chunks = tuple((off, min(FFN_CHUNK, dff - off)) for off in range(0, dff, FFN_CHUNK))
    kern = functools.partial(_ffn_kernel, sub=min(FFN_SUB, tm), chunks=chunks)
    ln = layer * N_LN + ln_idx
    return pl.pallas_call(
        kern,
        grid=(n // tm,),
        in_specs=[pl.BlockSpec((tm, d), lambda i: (i, 0)),
                  _layer_spec(d, dff, layer, 0), _layer_spec(d, dff, layer, 1),
                  _layer_spec(dff, d, layer),
                  _layer_spec(1, d, ln), _layer_spec(1, d, ln)],
        out_specs=pl.BlockSpec((tm, d), lambda i: (i, 0)),
        out_shape=jax.ShapeDtypeStruct((n, d), F32),
        compiler_params=_compiler_params(1),
        name="ffn",
    )(x, w_in, w_in, w_out, ln_g, ln_b)


def _conv_kernel(x_ref, xprev_ref, xnext_ref, wb_ref, wc_ref, wh_ref, ck_ref, wout_ref,
                 g_ref, b_ref, o_ref, v_ref, *, tiles_per_seq):
    i = pl.program_id(0)
    tm, d = x_ref.shape
    t = i % tiles_per_seq
    x = x_ref[...]
    xp = jnp.where(t == 0, 0.0, xprev_ref[...])
    xn = jnp.where(t == tiles_per_seq - 1, 0.0, xnext_ref[...])
    xcat = jnp.concatenate([xp, x, xn], axis=0).astype(BF16)
    cgate = jnp.dot(xcat, wc_ref[...], preferred_element_type=F32)
    hval = jnp.dot(xcat, wh_ref[...], preferred_element_type=F32)
    v_ref[...] = cgate * hval
    bgate = jnp.dot(x.astype(BF16), wb_ref[...], preferred_element_type=F32)
    ck = ck_ref[...]
    conv = (ck[0:1] * v_ref[SUBLANES - 1:SUBLANES - 1 + tm, :]
            + ck[1:2] * v_ref[SUBLANES:SUBLANES + tm, :]
            + ck[2:3] * v_ref[SUBLANES + 1:SUBLANES + 1 + tm, :])
    mix = jnp.dot((bgate * conv).astype(BF16), wout_ref[...], preferred_element_type=F32)
    o_ref[...] = _layer_norm(DEEPNORM_ALPHA * x + mix, g_ref[...], b_ref[...])


def _conv_mixer(x, w_in, ck, w_out, ln_g, ln_b, *, layer, mixer, seq, tm):
    n, d = x.shape
    rb = tm // SUBLANES
    last_rb = n // SUBLANES - 1
    kern = functools.partial(_conv_kernel, tiles_per_seq=seq // tm)
    ln = layer * N_LN + 1
    return pl.pallas_call(
        kern,
        grid=(n // tm,),
        in_specs=[pl.BlockSpec((tm, d), lambda i: (i, 0)),
                  pl.BlockSpec((SUBLANES, d), lambda i: (jnp.maximum(i * rb - 1, 0), 0)),
                  pl.BlockSpec((SUBLANES, d), lambda i: (jnp.minimum((i + 1) * rb, last_rb), 0)),
                  _layer_spec(d, d, mixer, 0), _layer_spec(d, d, mixer, 1),
                  _layer_spec(d, d, mixer, 2),
                  _layer_spec(CONV_WIDTH, d, mixer), _layer_spec(d, d, mixer),
                  _layer_spec(1, d, ln), _layer_spec(1, d, ln)],
        out_specs=pl.BlockSpec((tm, d), lambda i: (i, 0)),
        out_shape=jax.ShapeDtypeStruct((n, d), F32),
        scratch_shapes=[pltpu.VMEM((tm + 2 * SUBLANES, d), F32)],
        compiler_params=_compiler_params(1),
        name="conv_mixer",
    )(x, x, x, w_in, w_in, w_in, ck, w_out, ln_g, ln_b)


def _rope_tables(seq):
    t = jnp.arange(seq, dtype=jnp.int32)
    half = ROPE_AXIS_DIM // 2
    inv_freq = ROPE_THETA ** (-jnp.arange(0, ROPE_AXIS_DIM, 2, dtype=F32) / ROPE_AXIS_DIM)
    cos_parts, sin_parts = [], []
    for pos in (t // GRID_W, t % GRID_W):
        ang = pos.astype(F32)[:, None] * inv_freq[None, :]
        assert ang.shape == (seq, half)
        cos_parts += [jnp.cos(ang), jnp.cos(ang)]
        sin_parts += [-jnp.sin(ang), jnp.sin(ang)]
    return jnp.concatenate(cos_parts, axis=-1), jnp.concatenate(sin_parts, axis=-1)


def _qkv_kernel(x_ref, w_ref, qg_ref, kg_ref, cos_ref, sin_ref, q_ref, k_ref, vt_ref, *, sub):
    tk = vt_ref.shape[-1]
    half = ROPE_AXIS_DIM // 2
    lane = lax.broadcasted_iota(jnp.int32, (sub, HEAD_DIM), 1)
    first_half = (lane % ROPE_AXIS_DIM) < half
    qg = qg_ref[...] * (HEAD_DIM ** -0.5 * LOG2_E)
    kg = kg_ref[...]

    def project(r):
        return jnp.dot(x_ref[r * sub:(r + 1) * sub, :].astype(BF16), w_ref[...],
                       preferred_element_type=F32)

    def norm_rope_store(r, qkv):
        rows = slice(r * sub, (r + 1) * sub)
        cos = cos_ref[rows, :]
        sin = sin_ref[rows, :]

        def norm_rope(seg, gain):
            ms = jnp.mean(seg * seg, axis=-1, keepdims=True)
            y = seg * (lax.rsqrt(ms + QK_EPS) * gain)
            partner = jnp.where(first_half,
                                pltpu.roll(y, HEAD_DIM - half, axis=1),
                                pltpu.roll(y, half, axis=1))
            return (y * cos + partner * sin).astype(BF16)

        for h in range(N_HEADS):
            cols = slice(h * HEAD_DIM, (h + 1) * HEAD_DIM)
            q_ref[rows, cols] = norm_rope(qkv[:, cols], qg)
        for h in range(N_KV_HEADS):
            o = (N_HEADS + h) * HEAD_DIM
            k_ref[rows, h * HEAD_DIM:(h + 1) * HEAD_DIM] = norm_rope(qkv[:, o:o + HEAD_DIM], kg)
            o += N_KV_HEADS * HEAD_DIM
            blk, c = divmod(r * sub, tk)
            vt_ref[h, blk, :, c:c + sub] = qkv[:, o:o + HEAD_DIM].T.astype(BF16)

    _pipelined_rows(x_ref.shape[0] // sub, project, norm_rope_store)


def _qkv(x, w, qg, kg, cos, sin, *, mixer, seq, tm, tk):
    n, d = x.shape
    tiles_per_seq = seq // tm
    dq = N_HEADS * HEAD_DIM
    dkv = N_KV_HEADS * HEAD_DIM
    sub = min(PROJ_SUB, tk)
    assert tm % tk == 0 and tk % sub == 0
    kern = functools.partial(_qkv_kernel, sub=sub)
    return pl.pallas_call(
        kern,
        grid=(n // tm,),
        in_specs=[pl.BlockSpec((tm, d), lambda i: (i, 0)),
                  _layer_spec(d, dq + 2 * dkv, mixer),
                  _layer_spec(1, HEAD_DIM, mixer), _layer_spec(1, HEAD_DIM, mixer),
                  pl.BlockSpec((tm, HEAD_DIM), lambda i: (i % tiles_per_seq, 0)),
                  pl.BlockSpec((tm, HEAD_DIM), lambda i: (i % tiles_per_seq, 0))],
        out_specs=[pl.BlockSpec((tm, dq), lambda i: (i, 0)),
                   pl.BlockSpec((tm, dkv), lambda i: (i, 0)),
                   pl.BlockSpec((N_KV_HEADS, tm // tk, HEAD_DIM, tk), lambda i: (0, i, 0, 0))],
        out_shape=[jax.ShapeDtypeStruct((n, dq), BF16),
                   jax.ShapeDtypeStruct((n, dkv), BF16),
                   jax.ShapeDtypeStruct((N_KV_HEADS, n // tk, HEAD_DIM, tk), BF16)],
        compiler_params=_compiler_params(1),
        name="qkv_rope",
    )(x, w, qg, kg, cos, sin)


def _flash_kernel(q_ref, k_ref, vt_ref, o_ref, qt_ref, s_ref, acc_ref, *, tk):
    tq = q_ref.shape[0]
    pw = 2 * tq
    n_kv = k_ref.shape[0] // tk
    for h in range(GROUP):
        qh = q_ref[:, h * HEAD_DIM:(h + 1) * HEAD_DIM].astype(F32)
        qt_ref[:, h * tq:(h + 1) * tq] = qh.T.astype(BF16)
    acc_ref[...] = jnp.zeros(acc_ref.shape, F32)

    def scores(j, pair):
        kb = k_ref[pl.ds(pl.multiple_of(j * tk, tk), tk), :]
        s = jnp.dot(kb, qt_ref[:, pair * pw:(pair + 1) * pw], preferred_element_type=F32)
        s_ref[pair] = s
        return jnp.max(s, axis=0, keepdims=True)

    def softmax_pv(j, pair, mx, m, l):
        m_new = jnp.maximum(m, mx)
        alpha = jnp.exp2(m - m_new)
        p = jnp.exp2(s_ref[pair] - m_new)
        l_new = alpha * l + jnp.sum(p, axis=0, keepdims=True)
        acc_ref[pair] = alpha * acc_ref[pair] + jnp.dot(vt_ref[0, j], p.astype(BF16),
                                                        preferred_element_type=F32)
        return m_new, l_new

    def step(j, carry):
        mx0, m0, l0, m1, l1 = carry
        mx1 = scores(j, 1)
        m0, l0 = softmax_pv(j, 0, mx0, m0, l0)
        mx0 = scores(jnp.minimum(j + 1, n_kv - 1), 0)
        m1, l1 = softmax_pv(j, 1, mx1, m1, l1)
        return mx0, m0, l0, m1, l1

    neg_inf = jnp.full((1, pw), -jnp.inf, F32)
    zero = jnp.zeros((1, pw), F32)
    _, _, l0, _, l1 = lax.fori_loop(0, n_kv, step, (scores(0, 0), neg_inf, zero, neg_inf, zero),
                                    unroll=min(KV_UNROLL, n_kv))
    for h in range(GROUP):
        pair, c = divmod(h, 2)
        l = (l0, l1)[pair][:, c * tq:(c + 1) * tq]
        out = acc_ref[pair, :, c * tq:(c + 1) * tq] / l
        o_ref[:, h * HEAD_DIM:(h + 1) * HEAD_DIM] = out.T.astype(BF16)


def _flash(q, k, vt, *, batch, seq, tq, tk):
    n = q.shape[0]
    gw = GROUP * HEAD_DIM
    q_tiles = seq // tq
    n_kv = seq // tk
    assert vt.shape == (N_KV_HEADS, n // tk, HEAD_DIM, tk)
    kern = functools.partial(_flash_kernel, tk=tk)
    return pl.pallas_call(
        kern,
        grid=(batch, N_KV_HEADS, q_tiles),
        in_specs=[pl.BlockSpec((tq, gw), lambda b, g, i: (b * q_tiles + i, g)),
                  pl.BlockSpec((seq, HEAD_DIM), lambda b, g, i: (b, g)),
                  pl.BlockSpec((1, n_kv, HEAD_DIM, tk), lambda b, g, i: (g, b, 0, 0))],
        out_specs=pl.BlockSpec((tq, gw), lambda b, g, i: (b * q_tiles + i, g)),
        out_shape=jax.ShapeDtypeStruct((n, N_HEADS * HEAD_DIM), BF16),
        scratch_shapes=[pltpu.VMEM((HEAD_DIM, GROUP * tq), BF16),
                        pltpu.VMEM((GROUP // 2, tk, 2 * tq), F32),
                        pltpu.VMEM((GROUP // 2, HEAD_DIM, 2 * tq), F32)],
        compiler_params=_compiler_params(3),
        name="flash_gqa",
    )(q, k, vt)


def _proj_ln_kernel(a_ref, x_ref, w_ref, g_ref, b_ref, o_ref, *, sub):
    gamma = g_ref[...]
    beta = b_ref[...]

    def project(r):
        return jnp.dot(a_ref[r * sub:(r + 1) * sub, :], w_ref[...], preferred_element_type=F32)

    def residual_ln(r, mix):
        rows = slice(r * sub, (r + 1) * sub)
        o_ref[rows, :] = _layer_norm(DEEPNORM_ALPHA * x_ref[rows, :] + mix, gamma, beta)

    _pipelined_rows(x_ref.shape[0] // sub, project, residual_ln)


def _proj_ln(a, x, w, ln_g, ln_b, *, layer, mixer, tm):
    n, d = x.shape
    ln = layer * N_LN + 1
    kern = functools.partial(_proj_ln_kernel, sub=min(PROJ_SUB, tm))
    return pl.pallas_call(
        kern,
        grid=(n // tm,),
        in_specs=[pl.BlockSpec((tm, a.shape[1]), lambda i: (i, 0)),
                  pl.BlockSpec((tm, d), lambda i: (i, 0)),
                  _layer_spec(a.shape[1], d, mixer),
                  _layer_spec(1, d, ln), _layer_spec(1, d, ln)],
        out_specs=pl.BlockSpec((tm, d), lambda i: (i, 0)),
        out_shape=jax.ShapeDtypeStruct((n, d), F32),
        compiler_params=_compiler_params(1),
        name="attn_out_ln",
    )(a, x, w, ln_g, ln_b)


def _prepare_params(ffn1_w_in, ffn1_w_out, ffn2_w_in, ffn2_w_out, ln_g, ln_b, conv_w_in, conv_k,
                    conv_w_out, attn_w_qkv, attn_q_norm, attn_k_norm, attn_w_out):
    d = ln_g.shape[-1]
    return {
        "ffn_w_in": (ffn1_w_in.astype(BF16), ffn2_w_in.astype(BF16)),
        "ffn_w_out": (ffn1_w_out.astype(BF16), ffn2_w_out.astype(BF16)),
        "ln_g": ln_g.reshape(-1, 1, d), "ln_b": ln_b.reshape(-1, 1, d),
        "conv_w_in": conv_w_in.astype(BF16), "conv_k": conv_k,
        "conv_w_out": conv_w_out.astype(BF16),
        "attn_w_qkv": attn_w_qkv.astype(BF16),
        "attn_q_norm": attn_q_norm[:, None, :], "attn_k_norm": attn_k_norm[:, None, :],
        "attn_w_out": attn_w_out.astype(BF16),
    }


def _trunk(x, p):
    batch, seq, d = x.shape
    tm = min(TOKEN_TILE, seq)
    tc = min(CONV_TILE, seq)
    tq = min(Q_TILE, seq)
    tk = min(KV_TILE, seq)
    assert seq % tm == 0 and seq % tc == 0 and seq % tq == 0 and seq % tk == 0
    assert seq % GRID_W == 0
    x = x.reshape(batch * seq, d)
    cos, sin = _rope_tables(seq)
    ln_g, ln_b = p["ln_g"], p["ln_b"]
    for layer in range(DEPTH):
        x = _ffn(x, p["ffn_w_in"][0], p["ffn_w_out"][0], ln_g, ln_b, layer=layer, ln_idx=0, tm=tm)
        mixer = layer // 2
        if layer % 2 == 0:
            x = _conv_mixer(x, p["conv_w_in"], p["conv_k"], p["conv_w_out"], ln_g, ln_b,
                            layer=layer, mixer=mixer, seq=seq, tm=tc)
        else:
            q, k, vt = _qkv(x, p["attn_w_qkv"], p["attn_q_norm"], p["attn_k_norm"], cos, sin,
                            mixer=mixer, seq=seq, tm=tm, tk=tk)
            o = _flash(q, k, vt, batch=batch, seq=seq, tq=tq, tk=tk)
            x = _proj_ln(o, x, p["attn_w_out"], ln_g, ln_b, layer=layer, mixer=mixer, tm=tm)
        x = _ffn(x, p["ffn_w_in"][1], p["ffn_w_out"][1], ln_g, ln_b, layer=layer, ln_idx=2, tm=tm)
    return x.reshape(batch, seq, d)


def kernel(x_prompt, x_sample, ffn1_w_in, ffn1_w_out, ffn2_w_in, ffn2_w_out, ln_g, ln_b,
           conv_w_in, conv_k, conv_w_out, attn_w_qkv, attn_q_norm, attn_k_norm, attn_w_out):
    p = _prepare_params(ffn1_w_in, ffn1_w_out, ffn2_w_in, ffn2_w_out, ln_g, ln_b, conv_w_in,
                        conv_k, conv_w_out, attn_w_qkv, attn_q_norm, attn_k_norm, attn_w_out)
    return (_trunk(x_prompt, p), _trunk(x_sample, p))
```

```python
import functools

import jax
import jax.numpy as jnp
from jax import lax
from jax.experimental import pallas as pl
from jax.experimental.pallas import tpu as pltpu

DEPTH = 4
HEAD_DIM = 128
N_KV_HEADS = 2
GROUP = 4
N_HEADS = N_KV_HEADS * GROUP
ROPE_AXIS_DIM = HEAD_DIM // 2
ROPE_THETA = 10000.0
GRID_W = 64
CONV_WIDTH = 3
N_LN = 3
LN_EPS = 1e-5
QK_EPS = 1e-6
DEEPNORM_ALPHA = (2.0 * DEPTH) ** 0.25
LOG2_E = 1.4426950408889634

LANES = 128
SUBLANES = 8
MXU_DIM = 256
VMEM_LIMIT_BYTES = 56 * 1024 * 1024

TOKEN_TILE = 1024
PROJ_TILE = 1024
FFN_SUB = 512
PROJ_SUB = 256
CONV_TILE = 512
FFN_CHUNK = 2 * MXU_DIM
Q_TILE = 256
KV_TILE = 512
KV_UNROLL = 8

F32 = jnp.float32
BF16 = jnp.bfloat16


def _compiler_params(n_grid):
    return pltpu.CompilerParams(dimension_semantics=("arbitrary",) * n_grid,
                                vmem_limit_bytes=VMEM_LIMIT_BYTES)


def _layer_norm(y, g, b):
    mu = jnp.mean(y, axis=-1, keepdims=True)
    yc = y - mu
    var = jnp.mean(yc * yc, axis=-1, keepdims=True)
    return yc * lax.rsqrt(var + LN_EPS) * g + b


def _layer_spec(rows, cols, layer, col_block=0):
    return pl.BlockSpec((None, rows, cols), lambda *_: (layer, 0, col_block),
                        pipeline_mode=pl.Buffered(1))


def _pipelined_rows(n_sub, matmul_stage, vector_stage):
    pending = matmul_stage(0)
    for r in range(n_sub):
        nxt = matmul_stage(r + 1) if r + 1 < n_sub else None
        vector_stage(r, pending)
        pending = nxt


def _ffn_kernel(x_ref, wg_ref, wu_ref, wout_ref, g_ref, b_ref, o_ref, *, sub, chunks):
    gamma = g_ref[...]
    beta = b_ref[...]

    def swiglu(r):
        xb = x_ref[r * sub:(r + 1) * sub, :].astype(BF16)
        acc = None
        for off, width in chunks:
            gate = jnp.dot(xb, wg_ref[:, off:off + width], preferred_element_type=F32)
            up = jnp.dot(xb, wu_ref[:, off:off + width], preferred_element_type=F32)
            h = (gate / (1.0 + jnp.exp(-gate)) * up).astype(BF16)
            part = jnp.dot(h, wout_ref[off:off + width, :], preferred_element_type=F32)
            acc = part if acc is None else acc + part
        return acc

    def residual_ln(r, acc):
        rows = slice(r * sub, (r + 1) * sub)
        o_ref[rows, :] = _layer_norm(DEEPNORM_ALPHA * x_ref[rows, :] + 0.5 * acc, gamma, beta)

    _pipelined_rows(x_ref.shape[0] // sub, swiglu, residual_ln)


def _ffn(x, w_in, w_out, ln_g, ln_b, *, layer, ln_idx, tm):
    n, d = x.shape
    dff = w_out.shape[1]
    chunks = tuple((off, min(FFN_CHUNK, dff - off)) for off in range(0, dff, FFN_CHUNK))
    kern = functools.partial(_ffn_kernel, sub=min(FFN_SUB, tm), chunks=chunks)
    ln = layer * N_LN + ln_idx
    return pl.pallas_call(
        kern,
        grid=(n // tm,),
        in_specs=[pl.BlockSpec((tm, d), lambda i: (i, 0)),
                  _layer_spec(d, dff, layer, 0), _layer_spec(d, dff, layer, 1),
                  _layer_spec(dff, d, layer),
                  _layer_spec(1, d, ln), _layer_spec(1, d, ln)],
        out_specs=pl.BlockSpec((tm, d), lambda i: (i, 0)),
        out_shape=jax.ShapeDtypeStruct((n, d), F32),
        compiler_params=_compiler_params(1),
        name="ffn",
    )(x, w_in, w_in, w_out, ln_g, ln_b)


def _conv_kernel(x_ref, xprev_ref, xnext_ref, wb_ref, wc_ref, wh_ref, ck_ref, wout_ref,
                 g_ref, b_ref, o_ref, v_ref, *, tiles_per_seq):
    i = pl.program_id(0)
    tm, d = x_ref.shape
    t = i % tiles_per_seq
    x = x_ref[...]
    xp = jnp.where(t == 0, 0.0, xprev_ref[...])
    xn = jnp.where(t == tiles_per_seq - 1, 0.0, xnext_ref[...])
    xcat = jnp.concatenate([xp, x, xn], axis=0).astype(BF16)
    cgate = jnp.dot(xcat, wc_ref[...], preferred_element_type=F32)
    hval = jnp.dot(xcat, wh_ref[...], preferred_element_type=F32)
    v_ref[...] = cgate * hval
    bgate = jnp.dot(x.astype(BF16), wb_ref[...], preferred_element_type=F32)
    ck = ck_ref[...]
    conv = (ck[0:1] * v_ref[SUBLANES - 1:SUBLANES - 1 + tm, :]
            + ck[1:2] * v_ref[SUBLANES:SUBLANES + tm, :]
            + ck[2:3] * v_ref[SUBLANES + 1:SUBLANES + 1 + tm, :])
    mix = jnp.dot((bgate * conv).astype(BF16), wout_ref[...], preferred_element_type=F32)
    o_ref[...] = _layer_norm(DEEPNORM_ALPHA * x + mix, g_ref[...], b_ref[...])


def _conv_mixer(x, w_in, ck, w_out, ln_g, ln_b, *, layer, mixer, seq, tm):
    n, d = x.shape
    rb = tm // SUBLANES
    last_rb = n // SUBLANES - 1
    kern = functools.partial(_conv_kernel, tiles_per_seq=seq // tm)
    ln = layer * N_LN + 1
    return pl.pallas_call(
        kern,
        grid=(n // tm,),
        in_specs=[pl.BlockSpec((tm, d), lambda i: (i, 0)),
                  pl.BlockSpec((SUBLANES, d), lambda i: (jnp.maximum(i * rb - 1, 0), 0)),
                  pl.BlockSpec((SUBLANES, d), lambda i: (jnp.minimum((i + 1) * rb, last_rb), 0)),
                  _layer_spec(d, d, mixer, 0), _layer_spec(d, d, mixer, 1),
                  _layer_spec(d, d, mixer, 2),
                  _layer_spec(CONV_WIDTH, d, mixer), _layer_spec(d, d, mixer),
                  _layer_spec(1, d, ln), _layer_spec(1, d, ln)],
        out_specs=pl.BlockSpec((tm, d), lambda i: (i, 0)),
        out_shape=jax.ShapeDtypeStruct((n, d), F32),
        scratch_shapes=[pltpu.VMEM((tm + 2 * SUBLANES, d), F32)],
        compiler_params=_compiler_params(1),
        name="conv_mixer",
    )(x, x, x, w_in, w_in, w_in, ck, w_out, ln_g, ln_b)


def _pair_halves(w):
    lead = w.shape[:-1]
    w = w.reshape(*lead, -1, 2, 2, HEAD_DIM // 4)
    return jnp.swapaxes(w, -3, -2).reshape(*lead, -1)


def _rope_tables(seq):
    t = jnp.arange(seq, dtype=jnp.int32)
    inv_freq = ROPE_THETA ** (-jnp.arange(0, ROPE_AXIS_DIM, 2, dtype=F32) / ROPE_AXIS_DIM)
    ang = jnp.concatenate([pos.astype(F32)[:, None] * inv_freq[None, :]
                           for pos in (t // GRID_W, t % GRID_W)], axis=-1)
    cos = jnp.cos(ang)
    sin = jnp.sin(ang)
    return jnp.concatenate([cos, cos], axis=-1), jnp.concatenate([-sin, sin], axis=-1)


def _qkv_kernel(x_ref, w_ref, qg_ref, kg_ref, cos_ref, sin_ref, q_ref, k_ref, vt_ref, *, sub):
    tk = vt_ref.shape[-1]
    qg = qg_ref[...] * (HEAD_DIM ** -0.5 * LOG2_E)
    kg = kg_ref[...]

    def project(r):
        return jnp.dot(x_ref[r * sub:(r + 1) * sub, :].astype(BF16), w_ref[...],
                       preferred_element_type=F32)

    def norm_rope_store(r, qkv):
        rows = slice(r * sub, (r + 1) * sub)
        cos = cos_ref[rows, :]
        sin = sin_ref[rows, :]

        def norm_rope(seg, gain):
            ms = jnp.mean(seg * seg, axis=-1, keepdims=True)
            y = seg * (lax.rsqrt(ms + QK_EPS) * gain)
            partner = pltpu.roll(y, HEAD_DIM // 2, axis=1)
            return (y * cos + partner * sin).astype(BF16)

        for h in range(N_HEADS):
            cols = slice(h * HEAD_DIM, (h + 1) * HEAD_DIM)
            q_ref[rows, cols] = norm_rope(qkv[:, cols], qg)
        for h in range(N_KV_HEADS):
            o = (N_HEADS + h) * HEAD_DIM
            k_ref[rows, h * HEAD_DIM:(h + 1) * HEAD_DIM] = norm_rope(qkv[:, o:o + HEAD_DIM], kg)
            o += N_KV_HEADS * HEAD_DIM
            blk, c = divmod(r * sub, tk)
            vt_ref[h, blk, :, c:c + sub] = qkv[:, o:o + HEAD_DIM].T.astype(BF16)

    _pipelined_rows(x_ref.shape[0] // sub, project, norm_rope_store)


def _qkv(x, w, qg, kg, cos, sin, *, mixer, seq, tm, tk):
    n, d = x.shape
    tiles_per_seq = seq // tm
    dq = N_HEADS * HEAD_DIM
    dkv = N_KV_HEADS * HEAD_DIM
    sub = min(PROJ_SUB, tk)
    assert tm % tk == 0 and tk % sub == 0
    kern = functools.partial(_qkv_kernel, sub=sub)
    return pl.pallas_call(
        kern,
        grid=(n // tm,),
        in_specs=[pl.BlockSpec((tm, d), lambda i: (i, 0)),
                  _layer_spec(d, dq + 2 * dkv, mixer),
                  _layer_spec(1, HEAD_DIM, mixer), _layer_spec(1, HEAD_DIM, mixer),
                  pl.BlockSpec((tm, HEAD_DIM), lambda i: (i % tiles_per_seq, 0)),
                  pl.BlockSpec((tm, HEAD_DIM), lambda i: (i % tiles_per_seq, 0))],
        out_specs=[pl.BlockSpec((tm, dq), lambda i: (i, 0)),
                   pl.BlockSpec((tm, dkv), lambda i: (i, 0)),
                   pl.BlockSpec((N_KV_HEADS, tm // tk, HEAD_DIM, tk), lambda i: (0, i, 0, 0))],
        out_shape=[jax.ShapeDtypeStruct((n, dq), BF16),
                   jax.ShapeDtypeStruct((n, dkv), BF16),
                   jax.ShapeDtypeStruct((N_KV_HEADS, n // tk, HEAD_DIM, tk), BF16)],
        compiler_params=_compiler_params(1),
        name="qkv_rope",
    )(x, w, qg, kg, cos, sin)


def _flash_kernel(q_ref, k_ref, vt_ref, o_ref, qt_ref, s_ref, acc_ref, *, tk):
    tq = q_ref.shape[0]
    pw = 2 * tq
    n_kv = k_ref.shape[0] // tk
    for h in range(GROUP):
        qh = q_ref[:, h * HEAD_DIM:(h + 1) * HEAD_DIM].astype(F32)
        qt_ref[:, h * tq:(h + 1) * tq] = qh.T.astype(BF16)
    acc_ref[...] = jnp.zeros(acc_ref.shape, F32)

    def scores(j, pair):
        kb = k_ref[pl.ds(pl.multiple_of(j * tk, tk), tk), :]
        s = jnp.dot(kb, qt_ref[:, pair * pw:(pair + 1) * pw], preferred_element_type=F32)
        s_ref[pair] = s
        return jnp.max(s, axis=0, keepdims=True)

    def softmax_pv(j, pair, mx, m, l):
        m_new = jnp.maximum(m, mx)
        alpha = jnp.exp2(m - m_new)
        p = jnp.exp2(s_ref[pair] - m_new)
        l_new = alpha * l + jnp.sum(p, axis=0, keepdims=True)
        acc_ref[pair] = alpha * acc_ref[pair] + jnp.dot(vt_ref[0, j], p.astype(BF16),
                                                        preferred_element_type=F32)
        return m_new, l_new

    def step(j, carry):
        mx0, m0, l0, m1, l1 = carry
        mx1 = scores(j, 1)
        m0, l0 = softmax_pv(j, 0, mx0, m0, l0)
        mx0 = scores(jnp.minimum(j + 1, n_kv - 1), 0)
        m1, l1 = softmax_pv(j, 1, mx1, m1, l1)
        return mx0, m0, l0, m1, l1

    neg_inf = jnp.full((1, pw), -jnp.inf, F32)
    zero = jnp.zeros((1, pw), F32)
    _, _, l0, _, l1 = lax.fori_loop(0, n_kv, step, (scores(0, 0), neg_inf, zero, neg_inf, zero),
                                    unroll=min(KV_UNROLL, n_kv))
    for h in range(GROUP):
        pair, c = divmod(h, 2)
        l = (l0, l1)[pair][:, c * tq:(c + 1) * tq]
        out = acc_ref[pair, :, c * tq:(c + 1) * tq] / l
        o_ref[:, h * HEAD_DIM:(h + 1) * HEAD_DIM] = out.T.astype(BF16)


def _flash(q, k, vt, *, batch, seq, tq, tk):
    n = q.shape[0]
    gw = GROUP * HEAD_DIM
    q_tiles = seq // tq
    n_kv = seq // tk
    assert vt.shape == (N_KV_HEADS, n // tk, HEAD_DIM, tk)
    kern = functools.partial(_flash_kernel, tk=tk)
    return pl.pallas_call(
        kern,
        grid=(batch, N_KV_HEADS, q_tiles),
        in_specs=[pl.BlockSpec((tq, gw), lambda b, g, i: (b * q_tiles + i, g)),
                  pl.BlockSpec((seq, HEAD_DIM), lambda b, g, i: (b, g)),
                  pl.BlockSpec((1, n_kv, HEAD_DIM, tk), lambda b, g, i: (g, b, 0, 0))],
        out_specs=pl.BlockSpec((tq, gw), lambda b, g, i: (b * q_tiles + i, g)),
        out_shape=jax.ShapeDtypeStruct((n, N_HEADS * HEAD_DIM), BF16),
        scratch_shapes=[pltpu.VMEM((HEAD_DIM, GROUP * tq), BF16),
                        pltpu.VMEM((GROUP // 2, tk, 2 * tq), F32),
                        pltpu.VMEM((GROUP // 2, HEAD_DIM, 2 * tq), F32)],
        compiler_params=_compiler_params(3),
        name="flash_gqa",
    )(q, k, vt)


def _proj_ln_kernel(a_ref, x_ref, w_ref, g_ref, b_ref, o_ref, *, sub):
    gamma = g_ref[...]
    beta = b_ref[...]

    def project(r):
        return jnp.dot(a_ref[r * sub:(r + 1) * sub, :], w_ref[...], preferred_element_type=F32)

    def residual_ln(r, mix):
        rows = slice(r * sub, (r + 1) * sub)
        o_ref[rows, :] = _layer_norm(DEEPNORM_ALPHA * x_ref[rows, :] + mix, gamma, beta)

    _pipelined_rows(x_ref.shape[0] // sub, project, residual_ln)


def _proj_ln(a, x, w, ln_g, ln_b, *, layer, mixer, tm):
    n, d = x.shape
    ln = layer * N_LN + 1
    kern = functools.partial(_proj_ln_kernel, sub=min(PROJ_SUB, tm))
    return pl.pallas_call(
        kern,
        grid=(n // tm,),
        in_specs=[pl.BlockSpec((tm, a.shape[1]), lambda i: (i, 0)),
                  pl.BlockSpec((tm, d), lambda i: (i, 0)),
                  _layer_spec(a.shape[1], d, mixer),
                  _layer_spec(1, d, ln), _layer_spec(1, d, ln)],
        out_specs=pl.BlockSpec((tm, d), lambda i: (i, 0)),
        out_shape=jax.ShapeDtypeStruct((n, d), F32),
        compiler_params=_compiler_params(1),
        name="attn_out_ln",
    )(a, x, w, ln_g, ln_b)


def _prepare_params(ffn1_w_in, ffn1_w_out, ffn2_w_in, ffn2_w_out, ln_g, ln_b, conv_w_in, conv_k,
                    conv_w_out, attn_w_qkv, attn_q_norm, attn_k_norm, attn_w_out):
    d = ln_g.shape[-1]
    n_qk = (N_HEADS + N_KV_HEADS) * HEAD_DIM
    attn_w_qkv = jnp.concatenate([_pair_halves(attn_w_qkv[..., :n_qk]), attn_w_qkv[..., n_qk:]],
                                 axis=-1)
    attn_q_norm = _pair_halves(attn_q_norm)
    attn_k_norm = _pair_halves(attn_k_norm)
    return {
        "ffn_w_in": (ffn1_w_in.astype(BF16), ffn2_w_in.astype(BF16)),
        "ffn_w_out": (ffn1_w_out.astype(BF16), ffn2_w_out.astype(BF16)),
        "ln_g": ln_g.reshape(-1, 1, d), "ln_b": ln_b.reshape(-1, 1, d),
        "conv_w_in": conv_w_in.astype(BF16), "conv_k": conv_k,
        "conv_w_out": conv_w_out.astype(BF16),
        "attn_w_qkv": attn_w_qkv.astype(BF16),
        "attn_q_norm": attn_q_norm[:, None, :], "attn_k_norm": attn_k_norm[:, None, :],
        "attn_w_out": attn_w_out.astype(BF16),
    }


def _trunk(x, p):
    batch, seq, d = x.shape
    tm = min(TOKEN_TILE, seq)
    tp = min(PROJ_TILE, seq)
    tc = min(CONV_TILE, seq)
    tq = min(Q_TILE, seq)
    tk = min(KV_TILE, seq)
    assert seq % tm == 0 and seq % tc == 0 and seq % tq == 0 and seq % tk == 0 and seq % tp == 0
    assert seq % GRID_W == 0
    x = x.reshape(batch * seq, d)
    cos, sin = _rope_tables(seq)
    ln_g, ln_b = p["ln_g"], p["ln_b"]
    for layer in range(DEPTH):
        x = _ffn(x, p["ffn_w_in"][0], p["ffn_w_out"][0], ln_g, ln_b, layer=layer, ln_idx=0, tm=tm)
        mixer = layer // 2
        if layer % 2 == 0:
            x = _conv_mixer(x, p["conv_w_in"], p["conv_k"], p["conv_w_out"], ln_g, ln_b,
                            layer=layer, mixer=mixer, seq=seq, tm=tc)
        else:
            q, k, vt = _qkv(x, p["attn_w_qkv"], p["attn_q_norm"], p["attn_k_norm"], cos, sin,
                            mixer=mixer, seq=seq, tm=tp, tk=tk)
            o = _flash(q, k, vt, batch=batch, seq=seq, tq=tq, tk=tk)
            x = _proj_ln(o, x, p["attn_w_out"], ln_g, ln_b, layer=layer, mixer=mixer, tm=tp)
        x = _ffn(x, p["ffn_w_in"][1], p["ffn_w_out"][1], ln_g, ln_b, layer=layer, ln_idx=2, tm=tm)
    return x.reshape(batch, seq, d)


def kernel(x_prompt, x_sample, ffn1_w_in, ffn1_w_out, ffn2_w_in, ffn2_w_out, ln_g, ln_b,
           conv_w_in, conv_k, conv_w_out, attn_w_qkv, attn_q_norm, attn_k_norm, attn_w_out):
    p = _prepare_params(ffn1_w_in, ffn1_w_out, ffn2_w_in, ffn2_w_out, ln_g, ln_b, conv_w_in,
                        conv_k, conv_w_out, attn_w_qkv, attn_q_norm, attn_k_norm, attn_w_out)
    return (_trunk(x_prompt, p), _trunk(x_sample, p))
```

```python
import functools

import jax
import jax.numpy as jnp
from jax import lax
from jax.experimental import pallas as pl
from jax.experimental.pallas import tpu as pltpu

DEPTH = 4
HEAD_DIM = 128
N_KV_HEADS = 2
GROUP = 4
N_HEADS = N_KV_HEADS * GROUP
ROPE_AXIS_DIM = HEAD_DIM // 2
ROPE_THETA = 10000.0
GRID_W = 64
CONV_WIDTH = 3
N_LN = 3
LN_EPS = 1e-5
QK_EPS = 1e-6
DEEPNORM_ALPHA = (2.0 * DEPTH) ** 0.25
LOG2_E = 1.4426950408889634

LANES = 128
SUBLANES = 8
MXU_DIM = 256
VMEM_LIMIT_BYTES = 56 * 1024 * 1024

TOKEN_TILE = 1024
FFN_SUB = 512
CONV_TILE = 512
FFN_CHUNK = 2 * MXU_DIM
Q_TILE = 256
KV_TILE = 512
KV_UNROLL = 8

F32 = jnp.float32
BF16 = jnp.bfloat16


def _compiler_params(n_grid):
    return pltpu.CompilerParams(dimension_semantics=("arbitrary",) * n_grid,
                                vmem_limit_bytes=VMEM_LIMIT_BYTES)


def _layer_norm(y, g, b):
    mu = jnp.mean(y, axis=-1, keepdims=True)
    yc = y - mu
    var = jnp.mean(yc * yc, axis=-1, keepdims=True)
    return yc * lax.rsqrt(var + LN_EPS) * g + b


def _layer_spec(rows, cols, layer, col_block=0):
    return pl.BlockSpec((None, rows, cols), lambda *_: (layer, 0, col_block),
                        pipeline_mode=pl.Buffered(1))


def _qk_norm_rope(seg, gain, cos, sin):
    ms = jnp.mean(seg * seg, axis=-1, keepdims=True)
    y = seg * (lax.rsqrt(ms + QK_EPS) * gain)
    return (y * cos + pltpu.roll(y, HEAD_DIM // 2, axis=1) * sin).astype(BF16)


def _ffn_kernel(*refs, sub, chunks, attn_in, attn_out):
    refs = list(refs)

    def take(count):
        taken = refs[:count]
        del refs[:count]
        return taken

    if attn_in:
        a_ref, wproj_ref, pg_ref, pb_ref = take(4)
    x_ref, wg_ref, wu_ref, wout_ref, g_ref, b_ref = take(6)
    if attn_out:
        wqkv_ref, qg_ref, kg_ref, cos_ref, sin_ref = take(5)
    (o_ref,) = take(1)
    if attn_out:
        q_ref, k_ref, vt_ref = take(3)
    if attn_in:
        (xin_ref,) = take(1)
    assert not refs

    n_sub = x_ref.shape[0] // sub
    gamma = g_ref[...]
    beta = b_ref[...]

    def rows(r):
        return slice(r * sub, (r + 1) * sub)

    def ffn_input(r):
        return xin_ref[rows(r), :] if attn_in else x_ref[rows(r), :]

    def input_tasks(r):
        if not attn_in:
            return []

        def proj_ln():
            mix = jnp.dot(a_ref[rows(r), :], wproj_ref[...], preferred_element_type=F32)
            xin_ref[rows(r), :] = _layer_norm(DEEPNORM_ALPHA * x_ref[rows(r), :] + mix,
                                              pg_ref[...], pb_ref[...])
        return [proj_ln]

    def output_tasks(r, acc):
        state = {}

        def residual_ln():
            y = _layer_norm(DEEPNORM_ALPHA * ffn_input(r) + 0.5 * acc, gamma, beta)
            o_ref[rows(r), :] = y
            state["y"] = y

        if not attn_out:
            return [residual_ln]

        def project():
            state["qkv"] = jnp.dot(state.pop("y").astype(BF16), wqkv_ref[...],
                                   preferred_element_type=F32)

        def q_heads(h0, h1):
            def task():
                qg = qg_ref[...] * (HEAD_DIM ** -0.5 * LOG2_E)
                for h in range(h0, h1):
                    cols = slice(h * HEAD_DIM, (h + 1) * HEAD_DIM)
                    q_ref[rows(r), cols] = _qk_norm_rope(state["qkv"][:, cols], qg,
                                                         cos_ref[rows(r), :], sin_ref[rows(r), :])
            return task

        def kv_heads():
            qkv = state.pop("qkv")
            tk = vt_ref.shape[-1]
            blk, c = divmod(r * sub, tk)
            for h in range(N_KV_HEADS):
                o = (N_HEADS + h) * HEAD_DIM
                k_ref[rows(r), h * HEAD_DIM:(h + 1) * HEAD_DIM] = _qk_norm_rope(
                    qkv[:, o:o + HEAD_DIM], kg_ref[...], cos_ref[rows(r), :], sin_ref[rows(r), :])
                o += N_KV_HEADS * HEAD_DIM
                vt_ref[h, blk, :, c:c + sub] = qkv[:, o:o + HEAD_DIM].T.astype(BF16)

        return [residual_ln, project, q_heads(0, N_HEADS // 2), q_heads(N_HEADS // 2, N_HEADS),
                kv_heads]

    def swiglu(r, side_tasks):
        xb = ffn_input(r).astype(BF16)
        acc = None
        for off, width in chunks:
            gate = jnp.dot(xb, wg_ref[:, off:off + width], preferred_element_type=F32)
            up = jnp.dot(xb, wu_ref[:, off:off + width], preferred_element_type=F32)
            h = (gate / (1.0 + jnp.exp(-gate)) * up).astype(BF16)
            part = jnp.dot(h, wout_ref[off:off + width, :], preferred_element_type=F32)
            acc = part if acc is None else acc + part
            if side_tasks:
                side_tasks.pop(0)()
        for task in side_tasks:
            task()
        return acc

    for task in input_tasks(0):
        task()
    side_tasks = []
    for r in range(n_sub):
        if r + 1 < n_sub:
            side_tasks = side_tasks + input_tasks(r + 1)
        acc = swiglu(r, side_tasks)
        side_tasks = output_tasks(r, acc)
    for task in side_tasks:
        task()


def _ffn(x, p, *, which, layer, tm, seq=None, tk=None, attn_in=None, attn_out=False):
    n, d = x.shape
    w_in, w_out = p["ffn_w_in"][which], p["ffn_w_out"][which]
    dff = w_out.shape[1]
    chunks = tuple((off, min(FFN_CHUNK, dff - off)) for off in range(0, dff, FFN_CHUNK))
    mixer = layer // 2
    ln = layer * N_LN + 2 * which
    row_spec = pl.BlockSpec((tm, d), lambda i: (i, 0))
    operands, in_specs, scratch = [], [], []
    if attn_in is not None:
        da = attn_in.shape[1]
        operands += [attn_in, p["attn_w_out"], p["ln_g"], p["ln_b"]]
        in_specs += [pl.BlockSpec((tm, da), lambda i: (i, 0)), _layer_spec(da, d, mixer),
                     _layer_spec(1, d, layer * N_LN + 1), _layer_spec(1, d, layer * N_LN + 1)]
        scratch += [pltpu.VMEM((tm, d), F32)]
    operands += [x, w_in, w_in, w_out, p["ln_g"], p["ln_b"]]
    in_specs += [row_spec, _layer_spec(d, dff, layer, 0), _layer_spec(d, dff, layer, 1),
                 _layer_spec(dff, d, layer), _layer_spec(1, d, ln), _layer_spec(1, d, ln)]
    out_specs = [row_spec]
    out_shape = [jax.ShapeDtypeStruct((n, d), F32)]
    if attn_out:
        dq = N_HEADS * HEAD_DIM
        dkv = N_KV_HEADS * HEAD_DIM
        tiles_per_seq = seq // tm
        assert tm % tk == 0 and tk % min(FFN_SUB, tm) == 0
        cos, sin = p["rope"][seq]
        operands += [p["attn_w_qkv"], p["attn_q_norm"], p["attn_k_norm"], cos, sin]
        in_specs += [_layer_spec(d, dq + 2 * dkv, mixer),
                     _layer_spec(1, HEAD_DIM, mixer), _layer_spec(1, HEAD_DIM, mixer),
                     pl.BlockSpec((tm, HEAD_DIM), lambda i: (i % tiles_per_seq, 0)),
                     pl.BlockSpec((tm, HEAD_DIM), lambda i: (i % tiles_per_seq, 0))]
        out_specs += [pl.BlockSpec((tm, dq), lambda i: (i, 0)),
                      pl.BlockSpec((tm, dkv), lambda i: (i, 0)),
                      pl.BlockSpec((N_KV_HEADS, tm // tk, HEAD_DIM, tk), lambda i: (0, i, 0, 0))]
        out_shape += [jax.ShapeDtypeStruct((n, dq), BF16),
                      jax.ShapeDtypeStruct((n, dkv), BF16),
                      jax.ShapeDtypeStruct((N_KV_HEADS, n // tk, HEAD_DIM, tk), BF16)]
    kern = functools.partial(_ffn_kernel, sub=min(FFN_SUB, tm), chunks=chunks,
                             attn_in=attn_in is not None, attn_out=attn_out)
    out = pl.pallas_call(
        kern,
        grid=(n // tm,),
        in_specs=in_specs,
        out_specs=out_specs,
        out_shape=out_shape,
        scratch_shapes=scratch,
        compiler_params=_compiler_params(1),
        name="ffn",
    )(*operands)
    return out if attn_out else out[0]


def _conv_kernel(x_ref, xprev_ref, xnext_ref, wb_ref, wc_ref, wh_ref, ck_ref, wout_ref,
                 g_ref, b_ref, o_ref, v_ref, *, tiles_per_seq):
    i = pl.program_id(0)
    tm, d = x_ref.shape
    t = i % tiles_per_seq
    x = x_ref[...]
    xp = jnp.where(t == 0, 0.0, xprev_ref[...])
    xn = jnp.where(t == tiles_per_seq - 1, 0.0, xnext_ref[...])
    xcat = jnp.concatenate([xp, x, xn], axis=0).astype(BF16)
    cgate = jnp.dot(xcat, wc_ref[...], preferred_element_type=F32)
    hval = jnp.dot(xcat, wh_ref[...], preferred_element_type=F32)
    v_ref[...] = cgate * hval
    bgate = jnp.dot(x.astype(BF16), wb_ref[...], preferred_element_type=F32)
    ck = ck_ref[...]
    conv = (ck[0:1] * v_ref[SUBLANES - 1:SUBLANES - 1 + tm, :]
            + ck[1:2] * v_ref[SUBLANES:SUBLANES + tm, :]
            + ck[2:3] * v_ref[SUBLANES + 1:SUBLANES + 1 + tm, :])
    mix = jnp.dot((bgate * conv).astype(BF16), wout_ref[...], preferred_element_type=F32)
    o_ref[...] = _layer_norm(DEEPNORM_ALPHA * x + mix, g_ref[...], b_ref[...])


def _conv_mixer(x, w_in, ck, w_out, ln_g, ln_b, *, layer, mixer, seq, tm):
    n, d = x.shape
    rb = tm // SUBLANES
    last_rb = n // SUBLANES - 1
    kern = functools.partial(_conv_kernel, tiles_per_seq=seq // tm)
    ln = layer * N_LN + 1
    return pl.pallas_call(
        kern,
        grid=(n // tm,),
        in_specs=[pl.BlockSpec((tm, d), lambda i: (i, 0)),
                  pl.BlockSpec((SUBLANES, d), lambda i: (jnp.maximum(i * rb - 1, 0), 0)),
                  pl.BlockSpec((SUBLANES, d), lambda i: (jnp.minimum((i + 1) * rb, last_rb), 0)),
                  _layer_spec(d, d, mixer, 0), _layer_spec(d, d, mixer, 1),
                  _layer_spec(d, d, mixer, 2),
                  _layer_spec(CONV_WIDTH, d, mixer), _layer_spec(d, d, mixer),
                  _layer_spec(1, d, ln), _layer_spec(1, d, ln)],
        out_specs=pl.BlockSpec((tm, d), lambda i: (i, 0)),
        out_shape=jax.ShapeDtypeStruct((n, d), F32),
        scratch_shapes=[pltpu.VMEM((tm + 2 * SUBLANES, d), F32)],
        compiler_params=_compiler_params(1),
        name="conv_mixer",
    )(x, x, x, w_in, w_in, w_in, ck, w_out, ln_g, ln_b)


def _pair_halves(w):
    lead = w.shape[:-1]
    w = w.reshape(*lead, -1, 2, 2, HEAD_DIM // 4)
    return jnp.swapaxes(w, -3, -2).reshape(*lead, -1)


def _rope_tables(seq):
    t = jnp.arange(seq, dtype=jnp.int32)
    inv_freq = ROPE_THETA ** (-jnp.arange(0, ROPE_AXIS_DIM, 2, dtype=F32) / ROPE_AXIS_DIM)
    ang = jnp.concatenate([pos.astype(F32)[:, None] * inv_freq[None, :]
                           for pos in (t // GRID_W, t % GRID_W)], axis=-1)
    cos = jnp.cos(ang)
    sin = jnp.sin(ang)
    return jnp.concatenate([cos, cos], axis=-1), jnp.concatenate([-sin, sin], axis=-1)


def _flash_kernel(q_ref, k_ref, vt_ref, o_ref, qt_ref, s_ref, acc_ref, *, tk):
    tq = q_ref.shape[0]
    pw = 2 * tq
    n_kv = k_ref.shape[0] // tk
    for h in range(GROUP):
        qh = q_ref[:, h * HEAD_DIM:(h + 1) * HEAD_DIM].astype(F32)
        qt_ref[:, h * tq:(h + 1) * tq] = qh.T.astype(BF16)
    acc_ref[...] = jnp.zeros(acc_ref.shape, F32)

    def scores(j, pair):
        kb = k_ref[pl.ds(pl.multiple_of(j * tk, tk), tk), :]
        s = jnp.dot(kb, qt_ref[:, pair * pw:(pair + 1) * pw], preferred_element_type=F32)
        s_ref[pair] = s
        return jnp.max(s, axis=0, keepdims=True)

    def softmax_pv(j, pair, mx, m, l):
        m_new = jnp.maximum(m, mx)
        alpha = jnp.exp2(m - m_new)
        p = jnp.exp2(s_ref[pair] - m_new)
        l_new = alpha * l + jnp.sum(p, axis=0, keepdims=True)
        acc_ref[pair] = alpha * acc_ref[pair] + jnp.dot(vt_ref[0, j], p.astype(BF16),
                                                        preferred_element_type=F32)
        return m_new, l_new

    def step(j, carry):
        mx0, m0, l0, m1, l1 = carry
        mx1 = scores(j, 1)
        m0, l0 = softmax_pv(j, 0, mx0, m0, l0)
        mx0 = scores(jnp.minimum(j + 1, n_kv - 1), 0)
        m1, l1 = softmax_pv(j, 1, mx1, m1, l1)
        return mx0, m0, l0, m1, l1

    neg_inf = jnp.full((1, pw), -jnp.inf, F32)
    zero = jnp.zeros((1, pw), F32)
    _, _, l0, _, l1 = lax.fori_loop(0, n_kv, step, (scores(0, 0), neg_inf, zero, neg_inf, zero),
                                    unroll=min(KV_UNROLL, n_kv))
    for h in range(GROUP):
        pair, c = divmod(h, 2)
        l = (l0, l1)[pair][:, c * tq:(c + 1) * tq]
        out = acc_ref[pair, :, c * tq:(c + 1) * tq] / l
        o_ref[:, h * HEAD_DIM:(h + 1) * HEAD_DIM] = out.T.astype(BF16)


def _flash(q, k, vt, *, batch, seq, tq, tk):
    n = q.shape[0]
    gw = GROUP * HEAD_DIM
    q_tiles = seq // tq
    n_kv = seq // tk
    assert vt.shape == (N_KV_HEADS, n // tk, HEAD_DIM, tk)
    kern = functools.partial(_flash_kernel, tk=tk)
    return pl.pallas_call(
        kern,
        grid=(batch, N_KV_HEADS, q_tiles),
        in_specs=[pl.BlockSpec((tq, gw), lambda b, g, i: (b * q_tiles + i, g)),
                  pl.BlockSpec((seq, HEAD_DIM), lambda b, g, i: (b, g)),
                  pl.BlockSpec((1, n_kv, HEAD_DIM, tk), lambda b, g, i: (g, b, 0, 0))],
        out_specs=pl.BlockSpec((tq, gw), lambda b, g, i: (b * q_tiles + i, g)),
        out_shape=jax.ShapeDtypeStruct((n, N_HEADS * HEAD_DIM), BF16),
        scratch_shapes=[pltpu.VMEM((HEAD_DIM, GROUP * tq), BF16),
                        pltpu.VMEM((GROUP // 2, tk, 2 * tq), F32),
                        pltpu.VMEM((GROUP // 2, HEAD_DIM, 2 * tq), F32)],
        compiler_params=_compiler_params(3),
        name="flash_gqa",
    )(q, k, vt)


def _prepare_params(ffn1_w_in, ffn1_w_out, ffn2_w_in, ffn2_w_out, ln_g, ln_b, conv_w_in, conv_k,
                    conv_w_out, attn_w_qkv, attn_q_norm, attn_k_norm, attn_w_out):
    d = ln_g.shape[-1]
    n_qk = (N_HEADS + N_KV_HEADS) * HEAD_DIM
    attn_w_qkv = jnp.concatenate([_pair_halves(attn_w_qkv[..., :n_qk]), attn_w_qkv[..., n_qk:]],
                                 axis=-1)
    attn_q_norm = _pair_halves(attn_q_norm)
    attn_k_norm = _pair_halves(attn_k_norm)
    return {
        "ffn_w_in": (ffn1_w_in.astype(BF16), ffn2_w_in.astype(BF16)),
        "ffn_w_out": (ffn1_w_out.astype(BF16), ffn2_w_out.astype(BF16)),
        "ln_g": ln_g.reshape(-1, 1, d), "ln_b": ln_b.reshape(-1, 1, d),
        "conv_w_in": conv_w_in.astype(BF16), "conv_k": conv_k,
        "conv_w_out": conv_w_out.astype(BF16),
        "attn_w_qkv": attn_w_qkv.astype(BF16),
        "attn_q_norm": attn_q_norm[:, None, :], "attn_k_norm": attn_k_norm[:, None, :],
        "attn_w_out": attn_w_out.astype(BF16),
    }


def _rope_tables_for(seqs):
    cos, sin = _rope_tables(max(seqs))
    return {seq: (cos[:seq], sin[:seq]) for seq in seqs}


def _trunk(x, p):
    batch, seq, d = x.shape
    tm = min(TOKEN_TILE, seq)
    tc = min(CONV_TILE, seq)
    tq = min(Q_TILE, seq)
    tk = min(KV_TILE, seq)
    assert seq % tm == 0 and seq % tc == 0 and seq % tq == 0 and seq % tk == 0
    assert seq % GRID_W == 0
    x = x.reshape(batch * seq, d)
    for layer in range(DEPTH):
        if layer % 2 == 0:
            x = _ffn(x, p, which=0, layer=layer, tm=tm)
            x = _conv_mixer(x, p["conv_w_in"], p["conv_k"], p["conv_w_out"], p["ln_g"], p["ln_b"],
                            layer=layer, mixer=layer // 2, seq=seq, tm=tc)
            x = _ffn(x, p, which=1, layer=layer, tm=tm)
        else:
            x, q, k, vt = _ffn(x, p, which=0, layer=layer, tm=tm, seq=seq, tk=tk, attn_out=True)
            o = _flash(q, k, vt, batch=batch, seq=seq, tq=tq, tk=tk)
            x = _ffn(x, p, which=1, layer=layer, tm=tm, attn_in=o)
    return x.reshape(batch, seq, d)


def kernel(x_prompt, x_sample, ffn1_w_in, ffn1_w_out, ffn2_w_in, ffn2_w_out, ln_g, ln_b,
           conv_w_in, conv_k, conv_w_out, attn_w_qkv, attn_q_norm, attn_k_norm, attn_w_out):
    p = _prepare_params(ffn1_w_in, ffn1_w_out, ffn2_w_in, ffn2_w_out, ln_g, ln_b, conv_w_in,
                        conv_k, conv_w_out, attn_w_qkv, attn_q_norm, attn_k_norm, attn_w_out)
    p["rope"] = _rope_tables_for({x_prompt.shape[1], x_sample.shape[1]})
    return (_trunk(x_prompt, p), _trunk(x_sample, p))
```

```python
import functools

import jax
import jax.numpy as jnp
from jax import lax
from jax.experimental import pallas as pl
from jax.experimental.pallas import tpu as pltpu

DEPTH = 4
HEAD_DIM = 128
N_KV_HEADS = 2
GROUP = 4
N_HEADS = N_KV_HEADS * GROUP
ROPE_AXIS_DIM = HEAD_DIM // 2
ROPE_THETA = 10000.0
GRID_W = 64
CONV_WIDTH = 3
N_LN = 3
LN_EPS = 1e-5
QK_EPS = 1e-6
DEEPNORM_ALPHA = (2.0 * DEPTH) ** 0.25
LOG2_E = 1.4426950408889634

LANES = 128
SUBLANES = 8
MXU_DIM = 256
VMEM_LIMIT_BYTES = 56 * 1024 * 1024

TOKEN_TILE = 1024
FFN_TILE = 2048
FFN_SUB = 512
CONV_TILE = 512
FFN_CHUNK = 2 * MXU_DIM
Q_TILE = 256
KV_TILE = 1024
KV_UNROLL = 4

F32 = jnp.float32
BF16 = jnp.bfloat16


def _compiler_params(n_grid):
    return pltpu.CompilerParams(dimension_semantics=("arbitrary",) * n_grid,
                                vmem_limit_bytes=VMEM_LIMIT_BYTES)


def _layer_norm(y, g, b):
    mu = jnp.mean(y, axis=-1, keepdims=True)
    yc = y - mu
    var = jnp.mean(yc * yc, axis=-1, keepdims=True)
    return yc * lax.rsqrt(var + LN_EPS) * g + b


def _layer_spec(rows, cols, layer, col_block=0):
    return pl.BlockSpec((None, rows, cols), lambda *_: (layer, 0, col_block),
                        pipeline_mode=pl.Buffered(1))


def _qk_norm_rope(seg, gain, cos, sin):
    ms = jnp.mean(seg * seg, axis=-1, keepdims=True)
    y = seg * (lax.rsqrt(ms + QK_EPS) * gain)
    return (y * cos + pltpu.roll(y, HEAD_DIM // 2, axis=1) * sin).astype(BF16)


def _ffn_kernel(*refs, sub, chunks, attn_in, attn_out):
    refs = list(refs)

    def take(count):
        taken = refs[:count]
        del refs[:count]
        return taken

    if attn_in:
        a_ref, wproj_ref, pg_ref, pb_ref = take(4)
    x_ref, wg_ref, wu_ref, wout_ref, g_ref, b_ref = take(6)
    if attn_out:
        wqkv_ref, qg_ref, kg_ref, cos_ref, sin_ref = take(5)
    (o_ref,) = take(1)
    if attn_out:
        q_ref, k_ref, vt_ref = take(3)
    if attn_in:
        (xin_ref,) = take(1)
    assert not refs

    n_sub = x_ref.shape[0] // sub
    gamma = g_ref[...]
    beta = b_ref[...]

    def rows(r):
        return slice(r * sub, (r + 1) * sub)

    def ffn_input(r):
        return xin_ref[rows(r), :] if attn_in else x_ref[rows(r), :]

    def input_tasks(r):
        if not attn_in:
            return []

        def proj_ln():
            mix = jnp.dot(a_ref[rows(r), :], wproj_ref[...], preferred_element_type=F32)
            xin_ref[rows(r), :] = _layer_norm(DEEPNORM_ALPHA * x_ref[rows(r), :] + mix,
                                              pg_ref[...], pb_ref[...])
        return [proj_ln]

    def output_tasks(r, acc):
        state = {}

        def residual_ln():
            y = _layer_norm(DEEPNORM_ALPHA * ffn_input(r) + 0.5 * acc, gamma, beta)
            o_ref[rows(r), :] = y
            state["y"] = y

        if not attn_out:
            return [residual_ln]

        def project():
            state["qkv"] = jnp.dot(state.pop("y").astype(BF16), wqkv_ref[...],
                                   preferred_element_type=F32)

        def q_heads(h0, h1):
            def task():
                qg = qg_ref[...] * (HEAD_DIM ** -0.5 * LOG2_E)
                for h in range(h0, h1):
                    cols = slice(h * HEAD_DIM, (h + 1) * HEAD_DIM)
                    q_ref[rows(r), cols] = _qk_norm_rope(state["qkv"][:, cols], qg,
                                                         cos_ref[rows(r), :], sin_ref[rows(r), :])
            return task

        def kv_heads():
            qkv = state.pop("qkv")
            tk = vt_ref.shape[-1]
            blk, c = divmod(r * sub, tk)
            for h in range(N_KV_HEADS):
                o = (N_HEADS + h) * HEAD_DIM
                k_ref[rows(r), h * HEAD_DIM:(h + 1) * HEAD_DIM] = _qk_norm_rope(
                    qkv[:, o:o + HEAD_DIM], kg_ref[...], cos_ref[rows(r), :], sin_ref[rows(r), :])
                o += N_KV_HEADS * HEAD_DIM
                vt_ref[h, blk, :, c:c + sub] = qkv[:, o:o + HEAD_DIM].T.astype(BF16)

        return [residual_ln, project, q_heads(0, N_HEADS // 2), q_heads(N_HEADS // 2, N_HEADS),
                kv_heads]

    def swiglu(r, side_tasks):
        xb = ffn_input(r).astype(BF16)
        acc = None
        for off, width in chunks:
            gate = jnp.dot(xb, wg_ref[:, off:off + width], preferred_element_type=F32)
            up = jnp.dot(xb, wu_ref[:, off:off + width], preferred_element_type=F32)
            h = (gate / (1.0 + jnp.exp(-gate)) * up).astype(BF16)
            part = jnp.dot(h, wout_ref[off:off + width, :], preferred_element_type=F32)
            acc = part if acc is None else acc + part
            if side_tasks:
                side_tasks.pop(0)()
        for task in side_tasks:
            task()
        return acc

    for task in input_tasks(0):
        task()
    side_tasks = []
    for r in range(n_sub):
        if r + 1 < n_sub:
            side_tasks = side_tasks + input_tasks(r + 1)
        acc = swiglu(r, side_tasks)
        side_tasks = output_tasks(r, acc)
    for task in side_tasks:
        task()


def _ffn(x, p, *, which, layer, tm, seq=None, tk=None, attn_in=None, attn_out=False):
    n, d = x.shape
    w_in, w_out = p["ffn_w_in"][which], p["ffn_w_out"][which]
    dff = w_out.shape[1]
    chunks = tuple((off, min(FFN_CHUNK, dff - off)) for off in range(0, dff, FFN_CHUNK))
    mixer = layer // 2
    ln = layer * N_LN + 2 * which
    row_spec = pl.BlockSpec((tm, d), lambda i: (i, 0))
    operands, in_specs, scratch = [], [], []
    if attn_in is not None:
        da = attn_in.shape[1]
        operands += [attn_in, p["attn_w_out"], p["ln_g"], p["ln_b"]]
        in_specs += [pl.BlockSpec((tm, da), lambda i: (i, 0)), _layer_spec(da, d, mixer),
                     _layer_spec(1, d, layer * N_LN + 1), _layer_spec(1, d, layer * N_LN + 1)]
        scratch += [pltpu.VMEM((tm, d), F32)]
    operands += [x, w_in, w_in, w_out, p["ln_g"], p["ln_b"]]
    in_specs += [row_spec, _layer_spec(d, dff, layer, 0), _layer_spec(d, dff, layer, 1),
                 _layer_spec(dff, d, layer), _layer_spec(1, d, ln), _layer_spec(1, d, ln)]
    out_specs = [row_spec]
    out_shape = [jax.ShapeDtypeStruct((n, d), F32)]
    if attn_out:
        dq = N_HEADS * HEAD_DIM
        dkv = N_KV_HEADS * HEAD_DIM
        tiles_per_seq = seq // tm
        assert tm % tk == 0 and tk % min(FFN_SUB, tm) == 0
        cos, sin = p["rope"][seq]
        operands += [p["attn_w_qkv"], p["attn_q_norm"], p["attn_k_norm"], cos, sin]
        in_specs += [_layer_spec(d, dq + 2 * dkv, mixer),
                     _layer_spec(1, HEAD_DIM, mixer), _layer_spec(1, HEAD_DIM, mixer),
                     pl.BlockSpec((tm, HEAD_DIM), lambda i: (i % tiles_per_seq, 0)),
                     pl.BlockSpec((tm, HEAD_DIM), lambda i: (i % tiles_per_seq, 0))]
        out_specs += [pl.BlockSpec((tm, dq), lambda i: (i, 0)),
                      pl.BlockSpec((tm, dkv), lambda i: (i, 0)),
                      pl.BlockSpec((N_KV_HEADS, tm // tk, HEAD_DIM, tk), lambda i: (0, i, 0, 0))]
        out_shape += [jax.ShapeDtypeStruct((n, dq), BF16),
                      jax.ShapeDtypeStruct((n, dkv), BF16),
                      jax.ShapeDtypeStruct((N_KV_HEADS, n // tk, HEAD_DIM, tk), BF16)]
    kern = functools.partial(_ffn_kernel, sub=min(FFN_SUB, tm), chunks=chunks,
                             attn_in=attn_in is not None, attn_out=attn_out)
    out = pl.pallas_call(
        kern,
        grid=(n // tm,),
        in_specs=in_specs,
        out_specs=out_specs,
        out_shape=out_shape,
        scratch_shapes=scratch,
        compiler_params=_compiler_params(1),
        name="ffn",
    )(*operands)
    return out if attn_out else out[0]


def _conv_kernel(x_ref, xprev_ref, xnext_ref, wb_ref, wc_ref, wh_ref, ck_ref, wout_ref,
                 g_ref, b_ref, o_ref, v_ref, *, tiles_per_seq):
    i = pl.program_id(0)
    tm, d = x_ref.shape
    t = i % tiles_per_seq
    x = x_ref[...]
    xp = jnp.where(t == 0, 0.0, xprev_ref[...])
    xn = jnp.where(t == tiles_per_seq - 1, 0.0, xnext_ref[...])
    xcat = jnp.concatenate([xp, x, xn], axis=0).astype(BF16)
    cgate = jnp.dot(xcat, wc_ref[...], preferred_element_type=F32)
    hval = jnp.dot(xcat, wh_ref[...], preferred_element_type=F32)
    v_ref[...] = cgate * hval
    bgate = jnp.dot(x.astype(BF16), wb_ref[...], preferred_element_type=F32)
    ck = ck_ref[...]
    conv = (ck[0:1] * v_ref[SUBLANES - 1:SUBLANES - 1 + tm, :]
            + ck[1:2] * v_ref[SUBLANES:SUBLANES + tm, :]
            + ck[2:3] * v_ref[SUBLANES + 1:SUBLANES + 1 + tm, :])
    mix = jnp.dot((bgate * conv).astype(BF16), wout_ref[...], preferred_element_type=F32)
    o_ref[...] = _layer_norm(DEEPNORM_ALPHA * x + mix, g_ref[...], b_ref[...])


def _conv_mixer(x, w_in, ck, w_out, ln_g, ln_b, *, layer, mixer, seq, tm):
    n, d = x.shape
    rb = tm // SUBLANES
    last_rb = n // SUBLANES - 1
    kern = functools.partial(_conv_kernel, tiles_per_seq=seq // tm)
    ln = layer * N_LN + 1
    return pl.pallas_call(
        kern,
        grid=(n // tm,),
        in_specs=[pl.BlockSpec((tm, d), lambda i: (i, 0)),
                  pl.BlockSpec((SUBLANES, d), lambda i: (jnp.maximum(i * rb - 1, 0), 0)),
                  pl.BlockSpec((SUBLANES, d), lambda i: (jnp.minimum((i + 1) * rb, last_rb), 0)),
                  _layer_spec(d, d, mixer, 0), _layer_spec(d, d, mixer, 1),
                  _layer_spec(d, d, mixer, 2),
                  _layer_spec(CONV_WIDTH, d, mixer), _layer_spec(d, d, mixer),
                  _layer_spec(1, d, ln), _layer_spec(1, d, ln)],
        out_specs=pl.BlockSpec((tm, d), lambda i: (i, 0)),
        out_shape=jax.ShapeDtypeStruct((n, d), F32),
        scratch_shapes=[pltpu.VMEM((tm + 2 * SUBLANES, d), F32)],
        compiler_params=_compiler_params(1),
        name="conv_mixer",
    )(x, x, x, w_in, w_in, w_in, ck, w_out, ln_g, ln_b)


def _pair_halves(w):
    lead = w.shape[:-1]
    w = w.reshape(*lead, -1, 2, 2, HEAD_DIM // 4)
    return jnp.swapaxes(w, -3, -2).reshape(*lead, -1)


def _rope_tables(seq):
    t = jnp.arange(seq, dtype=jnp.int32)
    inv_freq = ROPE_THETA ** (-jnp.arange(0, ROPE_AXIS_DIM, 2, dtype=F32) / ROPE_AXIS_DIM)
    ang = jnp.concatenate([pos.astype(F32)[:, None] * inv_freq[None, :]
                           for pos in (t // GRID_W, t % GRID_W)], axis=-1)
    cos = jnp.cos(ang)
    sin = jnp.sin(ang)
    return jnp.concatenate([cos, cos], axis=-1), jnp.concatenate([-sin, sin], axis=-1)


def _flash_kernel(q_ref, k_ref, vt_ref, o_ref, qt_ref, s_ref, acc_ref, *, tk):
    tq = q_ref.shape[0]
    pw = 2 * tq
    n_kv = k_ref.shape[0] // tk

    def stage_qt(pair):
        for h in (2 * pair, 2 * pair + 1):
            qh = q_ref[:, h * HEAD_DIM:(h + 1) * HEAD_DIM].astype(F32)
            qt_ref[:, h * tq:(h + 1) * tq] = qh.T.astype(BF16)

    acc_ref[...] = jnp.zeros(acc_ref.shape, F32)

    def scores(j, pair):
        kb = k_ref[pl.ds(pl.multiple_of(j * tk, tk), tk), :]
        s = jnp.dot(kb, qt_ref[:, pair * pw:(pair + 1) * pw], preferred_element_type=F32)
        s_ref[pair] = s
        return jnp.max(s, axis=0, keepdims=True)

    def softmax_pv(j, pair, mx, m, l):
        m_new = jnp.maximum(m, mx)
        alpha = jnp.exp2(m - m_new)
        p = jnp.exp2(s_ref[pair] - m_new)
        l_new = alpha * l + jnp.sum(p, axis=0, keepdims=True)
        acc_ref[pair] = alpha * acc_ref[pair] + jnp.dot(vt_ref[0, j], p.astype(BF16),
                                                        preferred_element_type=F32)
        return m_new, l_new

    def step(j, carry):
        mx0, m0, l0, m1, l1 = carry
        mx1 = scores(j, 1)
        m0, l0 = softmax_pv(j, 0, mx0, m0, l0)
        mx0 = scores(jnp.minimum(j + 1, n_kv - 1), 0)
        m1, l1 = softmax_pv(j, 1, mx1, m1, l1)
        return mx0, m0, l0, m1, l1

    neg_inf = jnp.full((1, pw), -jnp.inf, F32)
    zero = jnp.zeros((1, pw), F32)
    stage_qt(0)
    mx0 = scores(0, 0)
    stage_qt(1)
    _, _, l0, _, l1 = lax.fori_loop(0, n_kv, step, (mx0, neg_inf, zero, neg_inf, zero),
                                    unroll=min(KV_UNROLL, n_kv))
    for h in range(GROUP):
        pair, c = divmod(h, 2)
        l = (l0, l1)[pair][:, c * tq:(c + 1) * tq]
        out = acc_ref[pair, :, c * tq:(c + 1) * tq] / l
        o_ref[:, h * HEAD_DIM:(h + 1) * HEAD_DIM] = out.T.astype(BF16)


def _flash(q, k, vt, *, batch, seq, tq, tk):
    n = q.shape[0]
    gw = GROUP * HEAD_DIM
    q_tiles = seq // tq
    n_kv = seq // tk
    assert vt.shape == (N_KV_HEADS, n // tk, HEAD_DIM, tk)
    kern = functools.partial(_flash_kernel, tk=tk)
    return pl.pallas_call(
        kern,
        grid=(batch, N_KV_HEADS, q_tiles),
        in_specs=[pl.BlockSpec((tq, gw), lambda b, g, i: (b * q_tiles + i, g)),
                  pl.BlockSpec((seq, HEAD_DIM), lambda b, g, i: (b, g)),
                  pl.BlockSpec((1, n_kv, HEAD_DIM, tk), lambda b, g, i: (g, b, 0, 0))],
        out_specs=pl.BlockSpec((tq, gw), lambda b, g, i: (b * q_tiles + i, g)),
        out_shape=jax.ShapeDtypeStruct((n, N_HEADS * HEAD_DIM), BF16),
        scratch_shapes=[pltpu.VMEM((HEAD_DIM, GROUP * tq), BF16),
                        pltpu.VMEM((GROUP // 2, tk, 2 * tq), F32),
                        pltpu.VMEM((GROUP // 2, HEAD_DIM, 2 * tq), F32)],
        compiler_params=_compiler_params(3),
        name="flash_gqa",
    )(q, k, vt)


def _prepare_params(ffn1_w_in, ffn1_w_out, ffn2_w_in, ffn2_w_out, ln_g, ln_b, conv_w_in, conv_k,
                    conv_w_out, attn_w_qkv, attn_q_norm, attn_k_norm, attn_w_out):
    d = ln_g.shape[-1]
    n_qk = (N_HEADS + N_KV_HEADS) * HEAD_DIM
    attn_w_qkv = jnp.concatenate([_pair_halves(attn_w_qkv[..., :n_qk]), attn_w_qkv[..., n_qk:]],
                                 axis=-1)
    attn_q_norm = _pair_halves(attn_q_norm)
    attn_k_norm = _pair_halves(attn_k_norm)
    return {
        "ffn_w_in": (ffn1_w_in.astype(BF16), ffn2_w_in.astype(BF16)),
        "ffn_w_out": (ffn1_w_out.astype(BF16), ffn2_w_out.astype(BF16)),
        "ln_g": ln_g.reshape(-1, 1, d), "ln_b": ln_b.reshape(-1, 1, d),
        "conv_w_in": conv_w_in.astype(BF16), "conv_k": conv_k,
        "conv_w_out": conv_w_out.astype(BF16),
        "attn_w_qkv": attn_w_qkv.astype(BF16),
        "attn_q_norm": attn_q_norm[:, None, :], "attn_k_norm": attn_k_norm[:, None, :],
        "attn_w_out": attn_w_out.astype(BF16),
    }


def _rope_tables_for(seqs):
    cos, sin = _rope_tables(max(seqs))
    return {seq: (cos[:seq], sin[:seq]) for seq in seqs}


def _trunk(x, p):
    batch, seq, d = x.shape
    tm = min(TOKEN_TILE, seq)
    tf = min(FFN_TILE, seq)
    tc = min(CONV_TILE, seq)
    tq = min(Q_TILE, seq)
    tk = min(KV_TILE, seq)
    assert seq % tm == 0 and seq % tc == 0 and seq % tq == 0 and seq % tk == 0 and seq % tf == 0
    assert seq % GRID_W == 0
    x = x.reshape(batch * seq, d)
    for layer in range(DEPTH):
        if layer % 2 == 0:
            x = _ffn(x, p, which=0, layer=layer, tm=tf)
            x = _conv_mixer(x, p["conv_w_in"], p["conv_k"], p["conv_w_out"], p["ln_g"], p["ln_b"],
                            layer=layer, mixer=layer // 2, seq=seq, tm=tc)
            x = _ffn(x, p, which=1, layer=layer, tm=tf)
        else:
            x, q, k, vt = _ffn(x, p, which=0, layer=layer, tm=tm, seq=seq, tk=tk, attn_out=True)
            o = _flash(q, k, vt, batch=batch, seq=seq, tq=tq, tk=tk)
            x = _ffn(x, p, which=1, layer=layer, tm=tm, attn_in=o)
    return x.reshape(batch, seq, d)


def kernel(x_prompt, x_sample, ffn1_w_in, ffn1_w_out, ffn2_w_in, ffn2_w_out, ln_g, ln_b,
           conv_w_in, conv_k, conv_w_out, attn_w_qkv, attn_q_norm, attn_k_norm, attn_w_out):
    p = _prepare_params(ffn1_w_in, ffn1_w_out, ffn2_w_in, ffn2_w_out, ln_g, ln_b, conv_w_in,
                        conv_k, conv_w_out, attn_w_qkv, attn_q_norm, attn_k_norm, attn_w_out)
    p["rope"] = _rope_tables_for({x_prompt.shape[1], x_sample.shape[1]})
    return (_trunk(x_prompt, p), _trunk(x_sample, p))
```

```python
import functools

import jax
import jax.numpy as jnp
from jax import lax
from jax.experimental import pallas as pl
from jax.experimental.pallas import tpu as pltpu

DEPTH = 4
HEAD_DIM = 128
N_KV_HEADS = 2
GROUP = 4
N_HEADS = N_KV_HEADS * GROUP
ROPE_AXIS_DIM = HEAD_DIM // 2
ROPE_THETA = 10000.0
GRID_W = 64
CONV_WIDTH = 3
N_LN = 3
LN_EPS = 1e-5
QK_EPS = 1e-6
DEEPNORM_ALPHA = (2.0 * DEPTH) ** 0.25
LOG2_E = 1.4426950408889634

LANES = 128
SUBLANES = 8
MXU_DIM = 256
VMEM_LIMIT_BYTES = 56 * 1024 * 1024

TOKEN_TILE = 1024
FFN_TILE = 1024
FFN_SUB = 512
CONV_TILE = 1024
FFN_CHUNK = 2 * MXU_DIM
Q_TILE = 256
KV_TILE = 1024
KV_UNROLL = 4

F32 = jnp.float32
BF16 = jnp.bfloat16


def _compiler_params(n_grid):
    return pltpu.CompilerParams(dimension_semantics=("arbitrary",) * n_grid,
                                vmem_limit_bytes=VMEM_LIMIT_BYTES)


def _layer_norm(y, g, b):
    mu = jnp.mean(y, axis=-1, keepdims=True)
    yc = y - mu
    var = jnp.mean(yc * yc, axis=-1, keepdims=True)
    return yc * lax.rsqrt(var + LN_EPS) * g + b


def _layer_spec(rows, cols, layer, col_block=0):
    return pl.BlockSpec((None, rows, cols), lambda *_: (layer, 0, col_block),
                        pipeline_mode=pl.Buffered(1))


def _qk_norm_rope(seg, gain, cos, sin):
    ms = jnp.mean(seg * seg, axis=-1, keepdims=True)
    y = seg * (lax.rsqrt(ms + QK_EPS) * gain)
    return (y * cos + pltpu.roll(y, HEAD_DIM // 2, axis=1) * sin).astype(BF16)


def _ffn_kernel(*refs, sub, chunks, attn_in, attn_out):
    refs = list(refs)

    def take(count):
        taken = refs[:count]
        del refs[:count]
        return taken

    if attn_in:
        a_ref, wproj_ref, pg_ref, pb_ref = take(4)
    x_ref, wg_ref, wu_ref, wout_ref, g_ref, b_ref = take(6)
    if attn_out:
        wqkv_ref, qg_ref, kg_ref, cos_ref, sin_ref = take(5)
    (o_ref,) = take(1)
    if attn_out:
        q_ref, k_ref, vt_ref = take(3)
    if attn_in:
        (xin_ref,) = take(1)
    assert not refs

    n_sub = x_ref.shape[0] // sub
    gamma = g_ref[...]
    beta = b_ref[...]

    def rows(r):
        return slice(r * sub, (r + 1) * sub)

    def ffn_input(r):
        return xin_ref[rows(r), :] if attn_in else x_ref[rows(r), :]

    def input_tasks(r):
        if not attn_in:
            return []

        def proj_ln():
            mix = jnp.dot(a_ref[rows(r), :], wproj_ref[...], preferred_element_type=F32)
            xin_ref[rows(r), :] = _layer_norm(DEEPNORM_ALPHA * x_ref[rows(r), :] + mix,
                                              pg_ref[...], pb_ref[...])
        return [proj_ln]

    def output_tasks(r, acc):
        state = {}

        def residual_ln():
            y = _layer_norm(DEEPNORM_ALPHA * ffn_input(r) + 0.5 * acc, gamma, beta)
            o_ref[rows(r), :] = y
            state["y"] = y

        if not attn_out:
            return [residual_ln]

        def project():
            state["qkv"] = jnp.dot(state.pop("y").astype(BF16), wqkv_ref[...],
                                   preferred_element_type=F32)

        def q_heads(h0, h1):
            def task():
                qg = qg_ref[...] * (HEAD_DIM ** -0.5 * LOG2_E)
                for h in range(h0, h1):
                    cols = slice(h * HEAD_DIM, (h + 1) * HEAD_DIM)
                    q_ref[rows(r), cols] = _qk_norm_rope(state["qkv"][:, cols], qg,
                                                         cos_ref[rows(r), :], sin_ref[rows(r), :])
            return task

        def kv_heads():
            qkv = state.pop("qkv")
            tk = vt_ref.shape[-1]
            blk, c = divmod(r * sub, tk)
            for h in range(N_KV_HEADS):
                o = (N_HEADS + h) * HEAD_DIM
                k_ref[rows(r), h * HEAD_DIM:(h + 1) * HEAD_DIM] = _qk_norm_rope(
                    qkv[:, o:o + HEAD_DIM], kg_ref[...], cos_ref[rows(r), :], sin_ref[rows(r), :])
                o += N_KV_HEADS * HEAD_DIM
                vt_ref[h, blk, :, c:c + sub] = qkv[:, o:o + HEAD_DIM].T.astype(BF16)

        return [residual_ln, project, q_heads(0, N_HEADS // 2), q_heads(N_HEADS // 2, N_HEADS),
                kv_heads]

    def swiglu(r, side_tasks):
        xb = ffn_input(r).astype(BF16)
        acc = None
        for off, width in chunks:
            gate = jnp.dot(xb, wg_ref[:, off:off + width], preferred_element_type=F32)
            up = jnp.dot(xb, wu_ref[:, off:off + width], preferred_element_type=F32)
            h = (gate / (1.0 + jnp.exp(-gate)) * up).astype(BF16)
            part = jnp.dot(h, wout_ref[off:off + width, :], preferred_element_type=F32)
            acc = part if acc is None else acc + part
            if side_tasks:
                side_tasks.pop(0)()
        for task in side_tasks:
            task()
        return acc

    for task in input_tasks(0):
        task()
    side_tasks = []
    for r in range(n_sub):
        if r + 1 < n_sub:
            side_tasks = side_tasks + input_tasks(r + 1)
        acc = swiglu(r, side_tasks)
        side_tasks = output_tasks(r, acc)
    for task in side_tasks:
        task()


def _ffn(x, p, *, which, layer, tm, seq=None, tk=None, attn_in=None, attn_out=False):
    n, d = x.shape
    w_in, w_out = p["ffn_w_in"][which], p["ffn_w_out"][which]
    dff = w_out.shape[1]
    chunks = tuple((off, min(FFN_CHUNK, dff - off)) for off in range(0, dff, FFN_CHUNK))
    mixer = layer // 2
    ln = layer * N_LN + 2 * which
    row_spec = pl.BlockSpec((tm, d), lambda i: (i, 0))
    operands, in_specs, scratch = [], [], []
    if attn_in is not None:
        da = attn_in.shape[1]
        operands += [attn_in, p["attn_w_out"], p["ln_g"], p["ln_b"]]
        in_specs += [pl.BlockSpec((tm, da), lambda i: (i, 0)), _layer_spec(da, d, mixer),
                     _layer_spec(1, d, layer * N_LN + 1), _layer_spec(1, d, layer * N_LN + 1)]
        scratch += [pltpu.VMEM((tm, d), F32)]
    operands += [x, w_in, w_in, w_out, p["ln_g"], p["ln_b"]]
    in_specs += [row_spec, _layer_spec(d, dff, layer, 0), _layer_spec(d, dff, layer, 1),
                 _layer_spec(dff, d, layer), _layer_spec(1, d, ln), _layer_spec(1, d, ln)]
    out_specs = [row_spec]
    out_shape = [jax.ShapeDtypeStruct((n, d), F32)]
    if attn_out:
        dq = N_HEADS * HEAD_DIM
        dkv = N_KV_HEADS * HEAD_DIM
        tiles_per_seq = seq // tm
        assert tm % tk == 0 and tk % min(FFN_SUB, tm) == 0
        cos, sin = p["rope"][seq]
        operands += [p["attn_w_qkv"], p["attn_q_norm"], p["attn_k_norm"], cos, sin]
        in_specs += [_layer_spec(d, dq + 2 * dkv, mixer),
                     _layer_spec(1, HEAD_DIM, mixer), _layer_spec(1, HEAD_DIM, mixer),
                     pl.BlockSpec((tm, HEAD_DIM), lambda i: (i % tiles_per_seq, 0)),
                     pl.BlockSpec((tm, HEAD_DIM), lambda i: (i % tiles_per_seq, 0))]
        out_specs += [pl.BlockSpec((tm, dq), lambda i: (i, 0)),
                      pl.BlockSpec((tm, dkv), lambda i: (i, 0)),
                      pl.BlockSpec((N_KV_HEADS, tm // tk, HEAD_DIM, tk), lambda i: (0, i, 0, 0))]
        out_shape += [jax.ShapeDtypeStruct((n, dq), BF16),
                      jax.ShapeDtypeStruct((n, dkv), BF16),
                      jax.ShapeDtypeStruct((N_KV_HEADS, n // tk, HEAD_DIM, tk), BF16)]
    kern = functools.partial(_ffn_kernel, sub=min(FFN_SUB, tm), chunks=chunks,
                             attn_in=attn_in is not None, attn_out=attn_out)
    out = pl.pallas_call(
        kern,
        grid=(n // tm,),
        in_specs=in_specs,
        out_specs=out_specs,
        out_shape=out_shape,
        scratch_shapes=scratch,
        compiler_params=_compiler_params(1),
        name="ffn",
    )(*operands)
    return out if attn_out else out[0]


def _conv_kernel(x_ref, xprev_ref, xnext_ref, wb_ref, wc_ref, wh_ref, ck_ref, wout_ref,
                 g_ref, b_ref, o_ref, v_ref, *, tiles_per_seq):
    i = pl.program_id(0)
    tm, d = x_ref.shape
    t = i % tiles_per_seq
    x = x_ref[...]
    xp = jnp.where(t == 0, 0.0, xprev_ref[...])
    xn = jnp.where(t == tiles_per_seq - 1, 0.0, xnext_ref[...])
    xcat = jnp.concatenate([xp, x, xn], axis=0).astype(BF16)
    cgate = jnp.dot(xcat, wc_ref[...], preferred_element_type=F32)
    hval = jnp.dot(xcat, wh_ref[...], preferred_element_type=F32)
    v_ref[...] = cgate * hval
    bgate = jnp.dot(x.astype(BF16), wb_ref[...], preferred_element_type=F32)
    ck = ck_ref[...]
    conv = (ck[0:1] * v_ref[SUBLANES - 1:SUBLANES - 1 + tm, :]
            + ck[1:2] * v_ref[SUBLANES:SUBLANES + tm, :]
            + ck[2:3] * v_ref[SUBLANES + 1:SUBLANES + 1 + tm, :])
    mix = jnp.dot((bgate * conv).astype(BF16), wout_ref[...], preferred_element_type=F32)
    o_ref[...] = _layer_norm(DEEPNORM_ALPHA * x + mix, g_ref[...], b_ref[...])


def _conv_mixer(x, w_in, ck, w_out, ln_g, ln_b, *, layer, mixer, seq, tm):
    n, d = x.shape
    rb = tm // SUBLANES
    last_rb = n // SUBLANES - 1
    kern = functools.partial(_conv_kernel, tiles_per_seq=seq // tm)
    ln = layer * N_LN + 1
    return pl.pallas_call(
        kern,
        grid=(n // tm,),
        in_specs=[pl.BlockSpec((tm, d), lambda i: (i, 0)),
                  pl.BlockSpec((SUBLANES, d), lambda i: (jnp.maximum(i * rb - 1, 0), 0)),
                  pl.BlockSpec((SUBLANES, d), lambda i: (jnp.minimum((i + 1) * rb, last_rb), 0)),
                  _layer_spec(d, d, mixer, 0), _layer_spec(d, d, mixer, 1),
                  _layer_spec(d, d, mixer, 2),
                  _layer_spec(CONV_WIDTH, d, mixer), _layer_spec(d, d, mixer),
                  _layer_spec(1, d, ln), _layer_spec(1, d, ln)],
        out_specs=pl.BlockSpec((tm, d), lambda i: (i, 0)),
        out_shape=jax.ShapeDtypeStruct((n, d), F32),
        scratch_shapes=[pltpu.VMEM((tm + 2 * SUBLANES, d), F32)],
        compiler_params=_compiler_params(1),
        name="conv_mixer",
    )(x, x, x, w_in, w_in, w_in, ck, w_out, ln_g, ln_b)


def _pair_halves(w):
    lead = w.shape[:-1]
    w = w.reshape(*lead, -1, 2, 2, HEAD_DIM // 4)
    return jnp.swapaxes(w, -3, -2).reshape(*lead, -1)


def _rope_tables(seq):
    t = jnp.arange(seq, dtype=jnp.int32)
    inv_freq = ROPE_THETA ** (-jnp.arange(0, ROPE_AXIS_DIM, 2, dtype=F32) / ROPE_AXIS_DIM)
    ang = jnp.concatenate([pos.astype(F32)[:, None] * inv_freq[None, :]
                           for pos in (t // GRID_W, t % GRID_W)], axis=-1)
    cos = jnp.cos(ang)
    sin = jnp.sin(ang)
    return jnp.concatenate([cos, cos], axis=-1), jnp.concatenate([-sin, sin], axis=-1)


def _flash_kernel(q_ref, k_ref, vt_ref, o_ref, qt_ref, s_ref, acc_ref, *, tk):
    tq = q_ref.shape[0]
    pw = 2 * tq
    n_kv = k_ref.shape[0] // tk

    def stage_qt(pair):
        for h in (2 * pair, 2 * pair + 1):
            qh = q_ref[:, h * HEAD_DIM:(h + 1) * HEAD_DIM].astype(F32)
            qt_ref[:, h * tq:(h + 1) * tq] = qh.T.astype(BF16)

    acc_ref[...] = jnp.zeros(acc_ref.shape, F32)

    def scores(j, pair):
        kb = k_ref[pl.ds(pl.multiple_of(j * tk, tk), tk), :]
        s = jnp.dot(kb, qt_ref[:, pair * pw:(pair + 1) * pw], preferred_element_type=F32)
        s_ref[pair] = s
        return jnp.max(s, axis=0, keepdims=True)

    def softmax_pv(j, pair, mx, m, l):
        m_new = jnp.maximum(m, mx)
        alpha = jnp.exp2(m - m_new)
        p = jnp.exp2(s_ref[pair] - m_new)
        l_new = alpha * l + jnp.sum(p, axis=0, keepdims=True)
        acc_ref[pair] = alpha * acc_ref[pair] + jnp.dot(vt_ref[0, j], p.astype(BF16),
                                                        preferred_element_type=F32)
        return m_new, l_new

    def step(j, carry):
        mx0, m0, l0, m1, l1 = carry
        mx1 = scores(j, 1)
        m0, l0 = softmax_pv(j, 0, mx0, m0, l0)
        mx0 = scores(jnp.minimum(j + 1, n_kv - 1), 0)
        m1, l1 = softmax_pv(j, 1, mx1, m1, l1)
        return mx0, m0, l0, m1, l1

    neg_inf = jnp.full((1, pw), -jnp.inf, F32)
    zero = jnp.zeros((1, pw), F32)
    stage_qt(0)
    mx0 = scores(0, 0)
    stage_qt(1)
    _, _, l0, _, l1 = lax.fori_loop(0, n_kv, step, (mx0, neg_inf, zero, neg_inf, zero),
                                    unroll=min(KV_UNROLL, n_kv))
    for h in range(GROUP):
        pair, c = divmod(h, 2)
        l = (l0, l1)[pair][:, c * tq:(c + 1) * tq]
        out = acc_ref[pair, :, c * tq:(c + 1) * tq] / l
        o_ref[:, h * HEAD_DIM:(h + 1) * HEAD_DIM] = out.T.astype(BF16)


def _flash(q, k, vt, *, batch, seq, tq, tk):
    n = q.shape[0]
    gw = GROUP * HEAD_DIM
    q_tiles = seq // tq
    n_kv = seq // tk
    assert vt.shape == (N_KV_HEADS, n // tk, HEAD_DIM, tk)
    kern = functools.partial(_flash_kernel, tk=tk)
    return pl.pallas_call(
        kern,
        grid=(batch, N_KV_HEADS, q_tiles),
        in_specs=[pl.BlockSpec((tq, gw), lambda b, g, i: (b * q_tiles + i, g)),
                  pl.BlockSpec((seq, HEAD_DIM), lambda b, g, i: (b, g)),
                  pl.BlockSpec((1, n_kv, HEAD_DIM, tk), lambda b, g, i: (g, b, 0, 0))],
        out_specs=pl.BlockSpec((tq, gw), lambda b, g, i: (b * q_tiles + i, g)),
        out_shape=jax.ShapeDtypeStruct((n, N_HEADS * HEAD_DIM), BF16),
        scratch_shapes=[pltpu.VMEM((HEAD_DIM, GROUP * tq), BF16),
                        pltpu.VMEM((GROUP // 2, tk, 2 * tq), F32),
                        pltpu.VMEM((GROUP // 2, HEAD_DIM, 2 * tq), F32)],
        compiler_params=_compiler_params(3),
        name="flash_gqa",
    )(q, k, vt)


def _prepare_params(ffn1_w_in, ffn1_w_out, ffn2_w_in, ffn2_w_out, ln_g, ln_b, conv_w_in, conv_k,
                    conv_w_out, attn_w_qkv, attn_q_norm, attn_k_norm, attn_w_out):
    d = ln_g.shape[-1]
    n_qk = (N_HEADS + N_KV_HEADS) * HEAD_DIM
    attn_w_qkv = jnp.concatenate([_pair_halves(attn_w_qkv[..., :n_qk]), attn_w_qkv[..., n_qk:]],
                                 axis=-1)
    attn_q_norm = _pair_halves(attn_q_norm)
    attn_k_norm = _pair_halves(attn_k_norm)
    return {
        "ffn_w_in": (ffn1_w_in.astype(BF16), ffn2_w_in.astype(BF16)),
        "ffn_w_out": (ffn1_w_out.astype(BF16), ffn2_w_out.astype(BF16)),
        "ln_g": ln_g.reshape(-1, 1, d), "ln_b": ln_b.reshape(-1, 1, d),
        "conv_w_in": conv_w_in.astype(BF16), "conv_k": conv_k,
        "conv_w_out": conv_w_out.astype(BF16),
        "attn_w_qkv": attn_w_qkv.astype(BF16),
        "attn_q_norm": attn_q_norm[:, None, :], "attn_k_norm": attn_k_norm[:, None, :],
        "attn_w_out": attn_w_out.astype(BF16),
    }


def _rope_tables_for(seqs):
    cos, sin = _rope_tables(max(seqs))
    return {seq: (cos[:seq], sin[:seq]) for seq in seqs}


def _trunk(x, p):
    batch, seq, d = x.shape
    tm = min(TOKEN_TILE, seq)
    tf = min(FFN_TILE, seq)
    tc = min(CONV_TILE, seq)
    tq = min(Q_TILE, seq)
    tk = min(KV_TILE, seq)
    assert seq % tm == 0 and seq % tc == 0 and seq % tq == 0 and seq % tk == 0 and seq % tf == 0
    assert seq % GRID_W == 0
    x = x.reshape(batch * seq, d)
    for layer in range(DEPTH):
        if layer % 2 == 0:
            x = _ffn(x, p, which=0, layer=layer, tm=tf)
            x = _conv_mixer(x, p["conv_w_in"], p["conv_k"], p["conv_w_out"], p["ln_g"], p["ln_b"],
                            layer=layer, mixer=layer // 2, seq=seq, tm=tc)
            x = _ffn(x, p, which=1, layer=layer, tm=tf)
        else:
            x, q, k, vt = _ffn(x, p, which=0, layer=layer, tm=tm, seq=seq, tk=tk, attn_out=True)
            o = _flash(q, k, vt, batch=batch, seq=seq, tq=tq, tk=tk)
            x = _ffn(x, p, which=1, layer=layer, tm=tm, attn_in=o)
    return x.reshape(batch, seq, d)


def kernel(x_prompt, x_sample, ffn1_w_in, ffn1_w_out, ffn2_w_in, ffn2_w_out, ln_g, ln_b,
           conv_w_in, conv_k, conv_w_out, attn_w_qkv, attn_q_norm, attn_k_norm, attn_w_out):
    p = _prepare_params(ffn1_w_in, ffn1_w_out, ffn2_w_in, ffn2_w_out, ln_g, ln_b, conv_w_in,
                        conv_k, conv_w_out, attn_w_qkv, attn_q_norm, attn_k_norm, attn_w_out)
    p["rope"] = _rope_tables_for({x_prompt.shape[1], x_sample.shape[1]})
    return (_trunk(x_prompt, p), _trunk(x_sample, p))
```

```python
import functools

import jax
import jax.numpy as jnp
from jax import lax
from jax.experimental import pallas as pl
from jax.experimental.pallas import tpu as pltpu

DEPTH = 4
HEAD_DIM = 128
N_KV_HEADS = 2
GROUP = 4
N_HEADS = N_KV_HEADS * GROUP
ROPE_AXIS_DIM = HEAD_DIM // 2
ROPE_THETA = 10000.0
GRID_W = 64
CONV_WIDTH = 3
N_LN = 3
LN_EPS = 1e-5
QK_EPS = 1e-6
DEEPNORM_ALPHA = (2.0 * DEPTH) ** 0.25
LOG2_E = 1.4426950408889634

LANES = 128
SUBLANES = 8
MXU_DIM = 256
VMEM_LIMIT_BYTES = 56 * 1024 * 1024

TOKEN_TILE = 1024
FFN_TILE = 1024
FFN_SUB = 512
CONV_TILE = 1024
FFN_CHUNK = 2 * MXU_DIM
Q_TILE = 512
KV_TILE = 512
KV_UNROLL = 4

F32 = jnp.float32
BF16 = jnp.bfloat16


def _compiler_params(n_grid):
    return pltpu.CompilerParams(dimension_semantics=("arbitrary",) * n_grid,
                                vmem_limit_bytes=VMEM_LIMIT_BYTES)


def _layer_norm(y, g, b):
    mu = jnp.mean(y, axis=-1, keepdims=True)
    yc = y - mu
    var = jnp.mean(yc * yc, axis=-1, keepdims=True)
    return yc * lax.rsqrt(var + LN_EPS) * g + b


def _layer_spec(rows, cols, layer, col_block=0):
    return pl.BlockSpec((None, rows, cols), lambda *_: (layer, 0, col_block),
                        pipeline_mode=pl.Buffered(1))


def _qk_norm_rope(seg, gain, cos, sin):
    ms = jnp.mean(seg * seg, axis=-1, keepdims=True)
    y = seg * (lax.rsqrt(ms + QK_EPS) * gain)
    return (y * cos + pltpu.roll(y, HEAD_DIM // 2, axis=1) * sin).astype(BF16)


def _ffn_kernel(*refs, sub, chunks, attn_in, attn_out):
    refs = list(refs)

    def take(count):
        taken = refs[:count]
        del refs[:count]
        return taken

    if attn_in:
        a_ref, wproj_ref, pg_ref, pb_ref = take(4)
    x_ref, wg_ref, wu_ref, wout_ref, g_ref, b_ref = take(6)
    if attn_out:
        wqkv_ref, qg_ref, kg_ref, cos_ref, sin_ref = take(5)
    (o_ref,) = take(1)
    if attn_out:
        q_ref, k_ref, vt_ref = take(3)
    if attn_in:
        (xin_ref,) = take(1)
    assert not refs

    n_sub = x_ref.shape[0] // sub
    gamma = g_ref[...]
    beta = b_ref[...]

    def rows(r):
        return slice(r * sub, (r + 1) * sub)

    def ffn_input(r):
        return xin_ref[rows(r), :] if attn_in else x_ref[rows(r), :]

    def input_tasks(r):
        if not attn_in:
            return []

        def proj_ln():
            mix = jnp.dot(a_ref[rows(r), :], wproj_ref[...], preferred_element_type=F32)
            xin_ref[rows(r), :] = _layer_norm(DEEPNORM_ALPHA * x_ref[rows(r), :] + mix,
                                              pg_ref[...], pb_ref[...])
        return [proj_ln]

    def output_tasks(r, acc):
        state = {}

        def residual_ln():
            y = _layer_norm(DEEPNORM_ALPHA * ffn_input(r) + 0.5 * acc, gamma, beta)
            o_ref[rows(r), :] = y
            state["y"] = y

        if not attn_out:
            return [residual_ln]

        def project():
            state["qkv"] = jnp.dot(state.pop("y").astype(BF16), wqkv_ref[...],
                                   preferred_element_type=F32)

        def q_heads(h0, h1):
            def task():
                qg = qg_ref[...] * (HEAD_DIM ** -0.5 * LOG2_E)
                for h in range(h0, h1):
                    cols = slice(h * HEAD_DIM, (h + 1) * HEAD_DIM)
                    q_ref[rows(r), cols] = _qk_norm_rope(state["qkv"][:, cols], qg,
                                                         cos_ref[rows(r), :], sin_ref[rows(r), :])
            return task

        def kv_heads():
            qkv = state.pop("qkv")
            tk = vt_ref.shape[-1]
            blk, c = divmod(r * sub, tk)
            for h in range(N_KV_HEADS):
                o = (N_HEADS + h) * HEAD_DIM
                k_ref[rows(r), h * HEAD_DIM:(h + 1) * HEAD_DIM] = _qk_norm_rope(
                    qkv[:, o:o + HEAD_DIM], kg_ref[...], cos_ref[rows(r), :], sin_ref[rows(r), :])
                o += N_KV_HEADS * HEAD_DIM
                vt_ref[h, blk, :, c:c + sub] = qkv[:, o:o + HEAD_DIM].T.astype(BF16)

        return [residual_ln, project, q_heads(0, N_HEADS // 2), q_heads(N_HEADS // 2, N_HEADS),
                kv_heads]

    def swiglu(r, side_tasks):
        xb = ffn_input(r).astype(BF16)
        acc = None
        for off, width in chunks:
            gate = jnp.dot(xb, wg_ref[:, off:off + width], preferred_element_type=F32)
            up = jnp.dot(xb, wu_ref[:, off:off + width], preferred_element_type=F32)
            h = (gate / (1.0 + jnp.exp(-gate)) * up).astype(BF16)
            part = jnp.dot(h, wout_ref[off:off + width, :], preferred_element_type=F32)
            acc = part if acc is None else acc + part
            if side_tasks:
                side_tasks.pop(0)()
        for task in side_tasks:
            task()
        return acc

    for task in input_tasks(0):
        task()
    side_tasks = []
    for r in range(n_sub):
        if r + 1 < n_sub:
            side_tasks = side_tasks + input_tasks(r + 1)
        acc = swiglu(r, side_tasks)
        side_tasks = output_tasks(r, acc)
    for task in side_tasks:
        task()


def _ffn(x, p, *, which, layer, tm, seq=None, tk=None, attn_in=None, attn_out=False):
    n, d = x.shape
    w_in, w_out = p["ffn_w_in"][which], p["ffn_w_out"][which]
    dff = w_out.shape[1]
    chunks = tuple((off, min(FFN_CHUNK, dff - off)) for off in range(0, dff, FFN_CHUNK))
    mixer = layer // 2
    ln = layer * N_LN + 2 * which
    row_spec = pl.BlockSpec((tm, d), lambda i: (i, 0))
    operands, in_specs, scratch = [], [], []
    if attn_in is not None:
        da = attn_in.shape[1]
        operands += [attn_in, p["attn_w_out"], p["ln_g"], p["ln_b"]]
        in_specs += [pl.BlockSpec((tm, da), lambda i: (i, 0)), _layer_spec(da, d, mixer),
                     _layer_spec(1, d, layer * N_LN + 1), _layer_spec(1, d, layer * N_LN + 1)]
        scratch += [pltpu.VMEM((tm, d), F32)]
    operands += [x, w_in, w_in, w_out, p["ln_g"], p["ln_b"]]
    in_specs += [row_spec, _layer_spec(d, dff, layer, 0), _layer_spec(d, dff, layer, 1),
                 _layer_spec(dff, d, layer), _layer_spec(1, d, ln), _layer_spec(1, d, ln)]
    out_specs = [row_spec]
    out_shape = [jax.ShapeDtypeStruct((n, d), F32)]
    if attn_out:
        dq = N_HEADS * HEAD_DIM
        dkv = N_KV_HEADS * HEAD_DIM
        tiles_per_seq = seq // tm
        assert tm % tk == 0 and tk % min(FFN_SUB, tm) == 0
        cos, sin = p["rope"][seq]
        operands += [p["attn_w_qkv"], p["attn_q_norm"], p["attn_k_norm"], cos, sin]
        in_specs += [_layer_spec(d, dq + 2 * dkv, mixer),
                     _layer_spec(1, HEAD_DIM, mixer), _layer_spec(1, HEAD_DIM, mixer),
                     pl.BlockSpec((tm, HEAD_DIM), lambda i: (i % tiles_per_seq, 0)),
                     pl.BlockSpec((tm, HEAD_DIM), lambda i: (i % tiles_per_seq, 0))]
        out_specs += [pl.BlockSpec((tm, dq), lambda i: (i, 0)),
                      pl.BlockSpec((tm, dkv), lambda i: (i, 0)),
                      pl.BlockSpec((N_KV_HEADS, tm // tk, HEAD_DIM, tk), lambda i: (0, i, 0, 0))]
        out_shape += [jax.ShapeDtypeStruct((n, dq), BF16),
                      jax.ShapeDtypeStruct((n, dkv), BF16),
                      jax.ShapeDtypeStruct((N_KV_HEADS, n // tk, HEAD_DIM, tk), BF16)]
    kern = functools.partial(_ffn_kernel, sub=min(FFN_SUB, tm), chunks=chunks,
                             attn_in=attn_in is not None, attn_out=attn_out)
    out = pl.pallas_call(
        kern,
        grid=(n // tm,),
        in_specs=in_specs,
        out_specs=out_specs,
        out_shape=out_shape,
        scratch_shapes=scratch,
        compiler_params=_compiler_params(1),
        name="ffn",
    )(*operands)
    return out if attn_out else out[0]


def _conv_kernel(x_ref, xprev_ref, xnext_ref, wb_ref, wc_ref, wh_ref, ck_ref, wout_ref,
                 g_ref, b_ref, o_ref, v_ref, *, tiles_per_seq):
    i = pl.program_id(0)
    tm, d = x_ref.shape
    t = i % tiles_per_seq
    x = x_ref[...]
    xp = jnp.where(t == 0, 0.0, xprev_ref[...])
    xn = jnp.where(t == tiles_per_seq - 1, 0.0, xnext_ref[...])
    xcat = jnp.concatenate([xp, x, xn], axis=0).astype(BF16)
    cgate = jnp.dot(xcat, wc_ref[...], preferred_element_type=F32)
    hval = jnp.dot(xcat, wh_ref[...], preferred_element_type=F32)
    v_ref[...] = cgate * hval
    bgate = jnp.dot(x.astype(BF16), wb_ref[...], preferred_element_type=F32)
    ck = ck_ref[...]
    conv = (ck[0:1] * v_ref[SUBLANES - 1:SUBLANES - 1 + tm, :]
            + ck[1:2] * v_ref[SUBLANES:SUBLANES + tm, :]
            + ck[2:3] * v_ref[SUBLANES + 1:SUBLANES + 1 + tm, :])
    mix = jnp.dot((bgate * conv).astype(BF16), wout_ref[...], preferred_element_type=F32)
    o_ref[...] = _layer_norm(DEEPNORM_ALPHA * x + mix, g_ref[...], b_ref[...])


def _conv_mixer(x, w_in, ck, w_out, ln_g, ln_b, *, layer, mixer, seq, tm):
    n, d = x.shape
    rb = tm // SUBLANES
    last_rb = n // SUBLANES - 1
    kern = functools.partial(_conv_kernel, tiles_per_seq=seq // tm)
    ln = layer * N_LN + 1
    return pl.pallas_call(
        kern,
        grid=(n // tm,),
        in_specs=[pl.BlockSpec((tm, d), lambda i: (i, 0)),
                  pl.BlockSpec((SUBLANES, d), lambda i: (jnp.maximum(i * rb - 1, 0), 0)),
                  pl.BlockSpec((SUBLANES, d), lambda i: (jnp.minimum((i + 1) * rb, last_rb), 0)),
                  _layer_spec(d, d, mixer, 0), _layer_spec(d, d, mixer, 1),
                  _layer_spec(d, d, mixer, 2),
                  _layer_spec(CONV_WIDTH, d, mixer), _layer_spec(d, d, mixer),
                  _layer_spec(1, d, ln), _layer_spec(1, d, ln)],
        out_specs=pl.BlockSpec((tm, d), lambda i: (i, 0)),
        out_shape=jax.ShapeDtypeStruct((n, d), F32),
        scratch_shapes=[pltpu.VMEM((tm + 2 * SUBLANES, d), F32)],
        compiler_params=_compiler_params(1),
        name="conv_mixer",
    )(x, x, x, w_in, w_in, w_in, ck, w_out, ln_g, ln_b)


def _pair_halves(w):
    lead = w.shape[:-1]
    w = w.reshape(*lead, -1, 2, 2, HEAD_DIM // 4)
    return jnp.swapaxes(w, -3, -2).reshape(*lead, -1)


def _rope_tables(seq):
    t = jnp.arange(seq, dtype=jnp.int32)
    inv_freq = ROPE_THETA ** (-jnp.arange(0, ROPE_AXIS_DIM, 2, dtype=F32) / ROPE_AXIS_DIM)
    ang = jnp.concatenate([pos.astype(F32)[:, None] * inv_freq[None, :]
                           for pos in (t // GRID_W, t % GRID_W)], axis=-1)
    cos = jnp.cos(ang)
    sin = jnp.sin(ang)
    return jnp.concatenate([cos, cos], axis=-1), jnp.concatenate([-sin, sin], axis=-1)


def _flash_kernel(q_ref, k_ref, vt_ref, o_ref, qt_ref, s_ref, acc_ref, *, tk):
    tq = q_ref.shape[0]
    pw = 2 * tq
    n_kv = k_ref.shape[0] // tk

    def stage_qt(pair):
        for h in (2 * pair, 2 * pair + 1):
            qh = q_ref[:, h * HEAD_DIM:(h + 1) * HEAD_DIM].astype(F32)
            qt_ref[:, h * tq:(h + 1) * tq] = qh.T.astype(BF16)

    acc_ref[...] = jnp.zeros(acc_ref.shape, F32)

    def scores(j, pair):
        kb = k_ref[pl.ds(pl.multiple_of(j * tk, tk), tk), :]
        s = jnp.dot(kb, qt_ref[:, pair * pw:(pair + 1) * pw], preferred_element_type=F32)
        s_ref[pair] = s
        return jnp.max(s, axis=0, keepdims=True)

    def softmax_pv(j, pair, mx, m, l):
        m_new = jnp.maximum(m, mx)
        alpha = jnp.exp2(m - m_new)
        p = jnp.exp2(s_ref[pair] - m_new)
        l_new = alpha * l + jnp.sum(p, axis=0, keepdims=True)
        acc_ref[pair] = alpha * acc_ref[pair] + jnp.dot(vt_ref[0, j], p.astype(BF16),
                                                        preferred_element_type=F32)
        return m_new, l_new

    def step(j, carry):
        mx0, m0, l0, m1, l1 = carry
        mx1 = scores(j, 1)
        m0, l0 = softmax_pv(j, 0, mx0, m0, l0)
        mx0 = scores(jnp.minimum(j + 1, n_kv - 1), 0)
        m1, l1 = softmax_pv(j, 1, mx1, m1, l1)
        return mx0, m0, l0, m1, l1

    neg_inf = jnp.full((1, pw), -jnp.inf, F32)
    zero = jnp.zeros((1, pw), F32)
    stage_qt(0)
    mx0 = scores(0, 0)
    stage_qt(1)
    _, _, l0, _, l1 = lax.fori_loop(0, n_kv, step, (mx0, neg_inf, zero, neg_inf, zero),
                                    unroll=min(KV_UNROLL, n_kv))
    for h in range(GROUP):
        pair, c = divmod(h, 2)
        l = (l0, l1)[pair][:, c * tq:(c + 1) * tq]
        out = acc_ref[pair, :, c * tq:(c + 1) * tq] / l
        o_ref[:, h * HEAD_DIM:(h + 1) * HEAD_DIM] = out.T.astype(BF16)


def _flash(q, k, vt, *, batch, seq, tq, tk):
    n = q.shape[0]
    gw = GROUP * HEAD_DIM
    q_tiles = seq // tq
    n_kv = seq // tk
    assert vt.shape == (N_KV_HEADS, n // tk, HEAD_DIM, tk)
    kern = functools.partial(_flash_kernel, tk=tk)
    return pl.pallas_call(
        kern,
        grid=(batch, N_KV_HEADS, q_tiles),
        in_specs=[pl.BlockSpec((tq, gw), lambda b, g, i: (b * q_tiles + i, g)),
                  pl.BlockSpec((seq, HEAD_DIM), lambda b, g, i: (b, g)),
                  pl.BlockSpec((1, n_kv, HEAD_DIM, tk), lambda b, g, i: (g, b, 0, 0))],
        out_specs=pl.BlockSpec((tq, gw), lambda b, g, i: (b * q_tiles + i, g)),
        out_shape=jax.ShapeDtypeStruct((n, N_HEADS * HEAD_DIM), BF16),
        scratch_shapes=[pltpu.VMEM((HEAD_DIM, GROUP * tq), BF16),
                        pltpu.VMEM((GROUP // 2, tk, 2 * tq), F32),
                        pltpu.VMEM((GROUP // 2, HEAD_DIM, 2 * tq), F32)],
        compiler_params=_compiler_params(3),
        name="flash_gqa",
    )(q, k, vt)


def _prepare_params(ffn1_w_in, ffn1_w_out, ffn2_w_in, ffn2_w_out, ln_g, ln_b, conv_w_in, conv_k,
                    conv_w_out, attn_w_qkv, attn_q_norm, attn_k_norm, attn_w_out):
    d = ln_g.shape[-1]
    n_qk = (N_HEADS + N_KV_HEADS) * HEAD_DIM
    attn_w_qkv = jnp.concatenate([_pair_halves(attn_w_qkv[..., :n_qk]), attn_w_qkv[..., n_qk:]],
                                 axis=-1)
    attn_q_norm = _pair_halves(attn_q_norm)
    attn_k_norm = _pair_halves(attn_k_norm)
    return {
        "ffn_w_in": (ffn1_w_in.astype(BF16), ffn2_w_in.astype(BF16)),
        "ffn_w_out": (ffn1_w_out.astype(BF16), ffn2_w_out.astype(BF16)),
        "ln_g": ln_g.reshape(-1, 1, d), "ln_b": ln_b.reshape(-1, 1, d),
        "conv_w_in": conv_w_in.astype(BF16), "conv_k": conv_k,
        "conv_w_out": conv_w_out.astype(BF16),
        "attn_w_qkv": attn_w_qkv.astype(BF16),
        "attn_q_norm": attn_q_norm[:, None, :], "attn_k_norm": attn_k_norm[:, None, :],
        "attn_w_out": attn_w_out.astype(BF16),
    }


def _rope_tables_for(seqs):
    cos, sin = _rope_tables(max(seqs))
    return {seq: (cos[:seq], sin[:seq]) for seq in seqs}


def _trunk(x, p):
    batch, seq, d = x.shape
    tm = min(TOKEN_TILE, seq)
    tf = min(FFN_TILE, seq)
    tc = min(CONV_TILE, seq)
    tq = min(Q_TILE, seq)
    tk = min(KV_TILE, seq)
    assert seq % tm == 0 and seq % tc == 0 and seq % tq == 0 and seq % tk == 0 and seq % tf == 0
    assert seq % GRID_W == 0
    x = x.reshape(batch * seq, d)
    for layer in range(DEPTH):
        if layer % 2 == 0:
            x = _ffn(x, p, which=0, layer=layer, tm=tf)
            x = _conv_mixer(x, p["conv_w_in"], p["conv_k"], p["conv_w_out"], p["ln_g"], p["ln_b"],
                            layer=layer, mixer=layer // 2, seq=seq, tm=tc)
            x = _ffn(x, p, which=1, layer=layer, tm=tf)
        else:
            x, q, k, vt = _ffn(x, p, which=0, layer=layer, tm=tm, seq=seq, tk=tk, attn_out=True)
            o = _flash(q, k, vt, batch=batch, seq=seq, tq=tq, tk=tk)
            x = _ffn(x, p, which=1, layer=layer, tm=tm, attn_in=o)
    return x.reshape(batch, seq, d)


def kernel(x_prompt, x_sample, ffn1_w_in, ffn1_w_out, ffn2_w_in, ffn2_w_out, ln_g, ln_b,
           conv_w_in, conv_k, conv_w_out, attn_w_qkv, attn_q_norm, attn_k_norm, attn_w_out):
    p = _prepare_params(ffn1_w_in, ffn1_w_out, ffn2_w_in, ffn2_w_out, ln_g, ln_b, conv_w_in,
                        conv_k, conv_w_out, attn_w_qkv, attn_q_norm, attn_k_norm, attn_w_out)
    p["rope"] = _rope_tables_for({x_prompt.shape[1], x_sample.shape[1]})
    return (_trunk(x_prompt, p), _trunk(x_sample, p))
```
